```python
import math
import jax, jax.numpy as jnp
from jax import lax
import numpy as np


D_MODEL = 4096
BATCH = 8
SEQ = 2048
DEPTH = 2
DEC_BATCH = 4
DEC_SEQ = 4096
PAST_LEN = 128

A_GROUPS = 4
A_GROUP_DIM = D_MODEL // 8
A_WIDTH = A_GROUPS * A_GROUP_DIM
B_HEADS = 4
B_DV = D_MODEL // 8
B_DK = B_DV // 2
B_QK_WIDTH = B_HEADS * B_DK
B_V_WIDTH = B_HEADS * B_DV
GATE_RANK = 16
GATE_TEMP = 16.0
GLA_CHUNK = 64
AB_IN = A_WIDTH + 2 * B_QK_WIDTH + 2 * B_V_WIDTH + 2 * GATE_RANK
AB_MIX = A_WIDTH + B_V_WIDTH
C_HEAD_DIM = 64
C_HEADS = D_MODEL // C_HEAD_DIM
C_KV_HEADS = 8
C_IN = (C_HEADS + 2 * C_KV_HEADS) * C_HEAD_DIM
WINDOW = 128
ATTN_BLOCK = 128
N_BUCKETS = 32
MAX_DISTANCE = 128
D_FF = -(-8 * D_MODEL // (3 * 256)) * 256
N_EVEN = (DEPTH + 1) // 2
N_ODD = DEPTH // 2
DEEPNORM_ALPHA = (2 * DEPTH) ** 0.25
DEEPNORM_BETA = (8 * DEPTH) ** -0.25
LN_EPS = 1e-5

kernel_name = 'hybrid_fnet_gla_swa_encoder'


def _standardize(xf):
    mu = jnp.mean(xf, axis=-1, keepdims=True)
    xc = xf - mu
    return xc * lax.rsqrt(jnp.mean(xc * xc, axis=-1, keepdims=True) + LN_EPS)


def layer_norm(x, g, b):
    y = _standardize(x.astype(jnp.float32)) * g.astype(jnp.float32) + b.astype(jnp.float32)
    return y.astype(x.dtype)


def group_norm(xf, g):
    return _standardize(xf.astype(jnp.float32)) * g.astype(jnp.float32)


def t5_bucket(rel):
    nb = N_BUCKETS // 2
    max_exact = nb // 2
    ret = jnp.where(rel > 0, nb, 0)
    n = jnp.abs(rel)
    nf = jnp.maximum(n, 1).astype(jnp.float32)
    large = max_exact + (jnp.log(nf / max_exact) / math.log(MAX_DISTANCE / max_exact)
                         * (nb - max_exact)).astype(jnp.int32)
    large = jnp.minimum(large, nb - 1)
    return ret + jnp.where(n < max_exact, n, large)


def band_bias(table):
    qq = jnp.arange(ATTN_BLOCK)[:, None]
    kk = jnp.arange(3 * ATTN_BLOCK)[None, :]
    rel = kk - ATTN_BLOCK - qq
    return jnp.transpose(table[t5_bucket(rel)], (2, 0, 1)).astype(jnp.float32)


def gla_chunked(q, k, v, log_a):
    B, S, H, dk = q.shape
    dv = v.shape[-1]
    nc = S // GLA_CHUNK

    def to_chunks(t):
        return t.reshape(B, nc, GLA_CHUNK, H, t.shape[-1]).transpose(1, 0, 3, 2, 4)

    qc, kc, vc = to_chunks(q), to_chunks(k), to_chunks(v)
    gc = jnp.cumsum(to_chunks(log_a), axis=3)
    tril = jnp.tril(jnp.ones((GLA_CHUNK, GLA_CHUNK), dtype=bool))

    def step(state, inp):
        q_, k_, v_, g_ = inp
        diff = g_[:, :, :, None, :] - g_[:, :, None, :, :]
        decay = jnp.exp(jnp.where(tril[:, :, None], diff, -jnp.inf))
        scores = jnp.einsum('bhid,bhijd->bhij', q_, decay * k_[:, :, None, :, :])
        o_intra = jnp.einsum('bhij,bhje->bhie', scores, v_)
        o_inter = jnp.einsum('bhid,bhde->bhie', q_ * jnp.exp(g_), state)
        g_last = g_[:, :, -1:, :]
        k_dec = k_ * jnp.exp(g_last - g_)
        state = state * jnp.swapaxes(jnp.exp(g_last), -1, -2) + jnp.einsum('bhjd,bhje->bhde', k_dec, v_)
        return state, o_intra + o_inter

    state0 = jnp.zeros((B, H, dk, dv), jnp.float32)
    _, ys = lax.scan(step, state0, (qc, kc, vc, gc))
    return ys.transpose(1, 0, 3, 2, 4).reshape(B, S, H, dv)


def mixer_ab(x, w_in, fourier_g, gate_w2, gate_b, head_norm_g, w_out):
    B, S, _ = x.shape
    proj = x @ w_in
    sizes = [A_WIDTH, B_QK_WIDTH, B_QK_WIDTH, B_V_WIDTH, B_V_WIDTH, GATE_RANK, GATE_RANK]
    cuts = np.cumsum(sizes)[:-1].tolist()
    u, q, k, v, r, g_fwd, g_bwd = jnp.split(proj, cuts, axis=-1)

    u = group_norm(u.reshape(B, S, A_GROUPS, A_GROUP_DIM), fourier_g)
    a_out = jnp.real(jnp.fft.fft2(u, axes=(1, 3), norm='ortho')).reshape(B, S, A_WIDTH)

    qf = q.astype(jnp.float32).reshape(B, S, B_HEADS, B_DK) * (B_DK ** -0.5)
    kf = k.astype(jnp.float32).reshape(B, S, B_HEADS, B_DK)
    vf = v.astype(jnp.float32).reshape(B, S, B_HEADS, B_DV)

    def log_gate(lr, d):
        z = lr.astype(jnp.float32) @ gate_w2[d].astype(jnp.float32) + gate_b[d].astype(jnp.float32)
        return (jax.nn.log_sigmoid(z) / GATE_TEMP).reshape(B, S, B_HEADS, B_DK)

    la_f = log_gate(g_fwd, 0)
    la_b = log_gate(g_bwd, 1)
    o_f = gla_chunked(qf, kf, vf, la_f)
    o_b = jnp.flip(gla_chunked(jnp.flip(qf, 1), jnp.flip(kf, 1), jnp.flip(vf, 1), jnp.flip(la_b, 1)), 1)
    o = group_norm(o_f + o_b, head_norm_g)
    o = o * jax.nn.silu(r.astype(jnp.float32).reshape(B, S, B_HEADS, B_DV))
    b_out = o.reshape(B, S, B_V_WIDTH)

    mix = jnp.concatenate([a_out, b_out], axis=-1).astype(x.dtype)
    return mix @ w_out


def window_attention(q, k, v, sinks, bias):
    B, S, Hq, dh = q.shape
    Hkv = k.shape[2]
    G = Hq // Hkv
    nb = S // ATTN_BLOCK
    scale = dh ** -0.5
    q = q.astype(jnp.float32)
    k = k.astype(jnp.float32)
    v = v.astype(jnp.float32)

    qb = q.reshape(B, nb, ATTN_BLOCK, Hkv, G, dh).transpose(1, 0, 3, 4, 2, 5)

    def key_blocks(t):
        tp = jnp.pad(t, ((0, 0), (ATTN_BLOCK, ATTN_BLOCK), (0, 0), (0, 0)))
        tp = tp.reshape(B, nb + 2, ATTN_BLOCK, Hkv, dh)
        tb = jnp.concatenate([tp[:, :-2], tp[:, 1:-1], tp[:, 2:]], axis=2)
        return tb.transpose(1, 0, 3, 2, 4)

    kb, vb = key_blocks(k), key_blocks(v)
    rel = jnp.arange(3 * ATTN_BLOCK)[None, :] - ATTN_BLOCK - jnp.arange(ATTN_BLOCK)[:, None]
    band = jnp.abs(rel) <= WINDOW
    bias_g = bias.reshape(Hkv, G, ATTN_BLOCK, 3 * ATTN_BLOCK)
    sink = sinks.astype(jnp.float32).reshape(Hkv, G)[None, :, :, None, None]

    def one_block(args):
        n, qn, kn, vn = args
        s = jnp.einsum('bkgqd,bkjd->bkgqj', qn, kn) * scale + bias_g
        kpos = n * ATTN_BLOCK - ATTN_BLOCK + jnp.arange(3 * ATTN_BLOCK)
        ok = band & ((kpos >= 0) & (kpos < S))[None, :]
        s = jnp.where(ok, s, -jnp.inf)
        m = jnp.maximum(jnp.max(s, axis=-1, keepdims=True), sink)
        p = jnp.exp(s - m)
        denom = jnp.sum(p, axis=-1, keepdims=True) + jnp.exp(sink - m)
        return jnp.einsum('bkgqj,bkjd->bkgqd', p, vn) / denom

    out = lax.map(one_block, (jnp.arange(nb), qb, kb, vb))
    return out.transpose(1, 0, 4, 2, 3, 5).reshape(B, S, Hq * dh)


def mixer_c(x, w_in, sinks, w_out, bias):
    B, S, _ = x.shape
    proj = x @ w_in
    qw = C_HEADS * C_HEAD_DIM
    kw = C_KV_HEADS * C_HEAD_DIM
    q = proj[..., :qw].reshape(B, S, C_HEADS, C_HEAD_DIM)
    k = proj[..., qw:qw + kw].reshape(B, S, C_KV_HEADS, C_HEAD_DIM)
    v = proj[..., qw + kw:].reshape(B, S, C_KV_HEADS, C_HEAD_DIM)
    o = window_attention(q, k, v, sinks, bias)
    return o.astype(x.dtype) @ w_out


def swiglu(x, w1, w3, w2):
    return (jax.nn.silu(x @ w1) * (x @ w3)) @ w2


def _normal(key, shape, scale):
    return jax.random.normal(key, shape, jnp.float32) * scale


def setup_inputs(seed: int = 0) -> dict:
    key = jax.random.key(seed)
    ks = jax.random.split(key, 18)
    return {
        'x_prompt': _normal(ks[0], (BATCH, SEQ, D_MODEL), 1.0),
        'x_sample': _normal(ks[1], (DEC_BATCH, DEC_SEQ, D_MODEL), 1.0),
        'rel_bias_table': _normal(ks[2], (N_BUCKETS, C_HEADS), 0.5),
        'ab_w_in': _normal(ks[3], (N_EVEN, D_MODEL, AB_IN), D_MODEL ** -0.5),
        'ab_fourier_g': 1.0 + _normal(ks[4], (N_EVEN, A_GROUPS, A_GROUP_DIM), 0.05),
        'ab_gate_w2': _normal(ks[5], (N_EVEN, 2, GATE_RANK, B_QK_WIDTH), GATE_RANK ** -0.5),
        'ab_gate_b': _normal(ks[6], (N_EVEN, 2, B_QK_WIDTH), 0.1),
        'ab_head_norm_g': 1.0 + _normal(ks[7], (N_EVEN, B_HEADS, B_DV), 0.05),
        'ab_w_out': _normal(ks[8], (N_EVEN, AB_MIX, D_MODEL), DEEPNORM_BETA * AB_MIX ** -0.5),
        'c_w_in': _normal(ks[9], (N_ODD, D_MODEL, C_IN), D_MODEL ** -0.5),
        'c_sinks': _normal(ks[10], (N_ODD, C_HEADS), 0.5),
        'c_w_out': _normal(ks[11], (N_ODD, C_HEADS * C_HEAD_DIM, D_MODEL), DEEPNORM_BETA * (C_HEADS * C_HEAD_DIM) ** -0.5),
        'ffn_w1': _normal(ks[12], (DEPTH, D_MODEL, D_FF), D_MODEL ** -0.5),
        'ffn_w3': _normal(ks[13], (DEPTH, D_MODEL, D_FF), D_MODEL ** -0.5),
        'ffn_w2': _normal(ks[14], (DEPTH, D_FF, D_MODEL), DEEPNORM_BETA * D_FF ** -0.5),
        'ln_g': 1.0 + _normal(ks[15], (DEPTH, 2, D_MODEL), 0.05),
        'ln_b': _normal(ks[16], (DEPTH, 2, D_MODEL), 0.02),
    }


def reference(x_prompt, x_sample, rel_bias_table, ab_w_in, ab_fourier_g, ab_gate_w2, ab_gate_b,
              ab_head_norm_g, ab_w_out, c_w_in, c_sinks, c_w_out, ffn_w1, ffn_w3, ffn_w2, ln_g, ln_b):
    bias = band_bias(rel_bias_table)

    def trunk(x):
        for i in range(DEPTH):
            j = i // 2
            if i % 2 == 0:
                h = mixer_ab(x, ab_w_in[j], ab_fourier_g[j], ab_gate_w2[j], ab_gate_b[j],
                             ab_head_norm_g[j], ab_w_out[j])
            else:
                h = mixer_c(x, c_w_in[j], c_sinks[j], c_w_out[j], bias)
            x = layer_norm(DEEPNORM_ALPHA * x + h, ln_g[i, 0], ln_b[i, 0])
            f = swiglu(x, ffn_w1[i], ffn_w3[i], ffn_w2[i])
            x = layer_norm(DEEPNORM_ALPHA * x + f, ln_g[i, 1], ln_b[i, 1])
        return x

    y_prompt = trunk(x_prompt)
    y_sample = trunk(x_sample)
    return (y_prompt, y_sample)
```

```python
import functools
import math

import jax
import jax.numpy as jnp
from jax import lax
from jax.experimental import pallas as pl
from jax.experimental.pallas import tpu as pltpu

D_MODEL = 4096
BATCH = 8
SEQ = 2048
DEPTH = 2
DEC_BATCH = 4
DEC_SEQ = 4096

A_GROUPS = 4
A_GROUP_DIM = D_MODEL // 8
A_WIDTH = A_GROUPS * A_GROUP_DIM
B_HEADS = 4
B_DV = D_MODEL // 8
B_DK = B_DV // 2
B_QK_WIDTH = B_HEADS * B_DK
B_V_WIDTH = B_HEADS * B_DV
GATE_RANK = 16
GATE_TEMP = 16.0
AB_MAIN = A_WIDTH + 2 * B_QK_WIDTH + 2 * B_V_WIDTH
C_HEAD_DIM = 64
C_HEADS = D_MODEL // C_HEAD_DIM
C_KV_HEADS = 8
C_GROUP = C_HEADS // C_KV_HEADS
C_Q_WIDTH = C_HEADS * C_HEAD_DIM
C_KV_WIDTH = C_KV_HEADS * C_HEAD_DIM
WINDOW = 128
ATTN_BLOCK = 128
N_BUCKETS = 32
MAX_DISTANCE = 128
D_FF = -(-8 * D_MODEL // (3 * 256)) * 256
DEEPNORM_ALPHA = (2 * DEPTH) ** 0.25
LN_EPS = 1e-5

T_PROMPT = BATCH * SEQ
T_SAMPLE = DEC_BATCH * DEC_SEQ
T_ALL = T_PROMPT + T_SAMPLE

F32 = jnp.float32
BF16 = jnp.bfloat16
NEG_BIG = -1e30

V7X_VMEM_LIMIT_BYTES = 56 * 1024 * 1024
LANES = 128
GLA_CHUNK = 64
GLA_SUB = 16


def _cparams(n_axes):
    return pltpu.CompilerParams(dimension_semantics=("arbitrary",) * n_axes,
                                vmem_limit_bytes=V7X_VMEM_LIMIT_BYTES)


def _fit(tile, dim):
    tile = min(tile, dim)
    assert dim % tile == 0, (tile, dim)
    return tile


def _dot(a, b):
    return jnp.dot(a, b, preferred_element_type=F32)


def _dot_nt(a, b):
    return lax.dot_general(a, b, (((1,), (1,)), ((), ())), preferred_element_type=F32)


def _dot_tn(a, b):
    return lax.dot_general(a, b, (((0,), (0,)), ((), ())), preferred_element_type=F32)


def _standardize(v):
    mu = jnp.mean(v, axis=-1, keepdims=True)
    c = v - mu
    return c * lax.rsqrt(jnp.mean(c * c, axis=-1, keepdims=True) + LN_EPS)


def _sigmoid(v):
    return 1.0 / (1.0 + jnp.exp(-v))


def _mm_body(x_ref, w_ref, o_ref):
    o_ref[...] = _dot(x_ref[...], w_ref[...]).astype(o_ref.dtype)


def _matmul(x, w, n_cols, *, tm, tn, out_dtype, name, col0=0):
    m, k = x.shape
    tm, tn = _fit(tm, m), _fit(tn, n_cols)
    cb = col0 // tn
    return pl.pallas_call(
        _mm_body,
        grid=(m // tm, n_cols // tn),
        in_specs=[pl.BlockSpec((tm, k), lambda i, j: (i, 0)),
                  pl.BlockSpec((k, tn), lambda i, j: (0, j + cb))],
        out_specs=pl.BlockSpec((tm, tn), lambda i, j: (i, j)),
        out_shape=jax.ShapeDtypeStruct((m, n_cols), out_dtype),
        compiler_params=_cparams(2),
        name=name,
    )(x, w)


def _mm2_body(a_ref, b_ref, w_ref, o_ref):
    ka = a_ref.shape[1]
    acc = _dot(a_ref[...], w_ref[:ka, :]) + _dot(b_ref[...], w_ref[ka:, :])
    o_ref[...] = acc.astype(o_ref.dtype)


def _matmul_split_lhs(a, b, w, *, tm, tn, name):
    m, ka = a.shape
    kb = b.shape[1]
    n = w.shape[1]
    tm, tn = _fit(tm, m), _fit(tn, n)
    return pl.pallas_call(
        _mm2_body,
        grid=(m // tm, n // tn),
        in_specs=[pl.BlockSpec((tm, ka), lambda i, j: (i, 0)),
                  pl.BlockSpec((tm, kb), lambda i, j: (i, 0)),
                  pl.BlockSpec((ka + kb, tn), lambda i, j: (0, j))],
        out_specs=pl.BlockSpec((tm, tn), lambda i, j: (i, j)),
        out_shape=jax.ShapeDtypeStruct((m, n), F32),
        compiler_params=_cparams(2),
        name=name,
    )(a, b, w)


def _swiglu_body(x_ref, w1_ref, w3_ref, o_ref):
    x = x_ref[...]
    a = _dot(x, w1_ref[...])
    b = _dot(x, w3_ref[...])
    o_ref[...] = (a * _sigmoid(a) * b).astype(o_ref.dtype)


def _swiglu_up(x, w1, w3, *, tm, tn):
    m, k = x.shape
    n = w1.shape[1]
    tm, tn = _fit(tm, m), _fit(tn, n)
    return pl.pallas_call(
        _swiglu_body,
        grid=(m // tm, n // tn),
        in_specs=[pl.BlockSpec((tm, k), lambda i, j: (i, 0)),
                  pl.BlockSpec((k, tn), lambda i, j: (0, j)),
                  pl.BlockSpec((k, tn), lambda i, j: (0, j))],
        out_specs=pl.BlockSpec((tm, tn), lambda i, j: (i, j)),
        out_shape=jax.ShapeDtypeStruct((m, n), BF16),
        compiler_params=_cparams(2),
        name="swiglu_up",
    )(x, w1, w3)


def _ln_body(x_ref, h_ref, g_ref, b_ref, y_ref, yb_ref):
    y = _standardize(DEEPNORM_ALPHA * x_ref[...] + h_ref[...]) * g_ref[...] + b_ref[...]
    y_ref[...] = y
    yb_ref[...] = y.astype(BF16)


def _deepnorm(x, h, g, b, *, tm):
    m, d = x.shape
    tm = _fit(tm, m)
    row = pl.BlockSpec((tm, d), lambda i: (i, 0))
    vec = pl.BlockSpec((1, d), lambda i: (0, 0))
    return pl.pallas_call(
        _ln_body,
        grid=(m // tm,),
        in_specs=[row, row, vec, vec],
        out_specs=[row, row],
        out_shape=[jax.ShapeDtypeStruct((m, d), F32), jax.ShapeDtypeStruct((m, d), BF16)],
        compiler_params=_cparams(1),
        name="deepnorm",
    )(x, h, g.reshape(1, d), b.reshape(1, d))


def _dft_body(c_ref, s_ref, *, n, scale, sin_sign):
    tr, nc = c_ref.shape
    t = lax.broadcasted_iota(jnp.int32, (tr, nc), 0) + pl.program_id(0) * tr
    s = lax.broadcasted_iota(jnp.int32, (tr, nc), 1)
    ang = ((t * s) & (n - 1)).astype(F32) * (2.0 * math.pi / n)
    c_ref[...] = (jnp.cos(ang) * scale).astype(BF16)
    s_ref[...] = (jnp.sin(ang) * (sin_sign * scale)).astype(BF16)


def _dft_matrices(n, *, sin_sign, tr):
    assert n & (n - 1) == 0
    blk = pl.BlockSpec((tr, n), lambda i: (i, 0))
    return pl.pallas_call(
        functools.partial(_dft_body, n=n, scale=n ** -0.5, sin_sign=sin_sign),
        grid=(n // tr,),
        out_specs=[blk, blk],
        out_shape=[jax.ShapeDtypeStruct((n, n), BF16)] * 2,
        compiler_params=_cparams(1),
        name=f"dft_matrices_{n}",
    )()


def _fnet_channel_body(u_ref, g_ref, c_ref, s_ref, p_ref, q_ref):
    for grp in range(A_GROUPS):
        sl = slice(grp * A_GROUP_DIM, (grp + 1) * A_GROUP_DIM)
        un = (_standardize(u_ref[:, sl]) * g_ref[:, sl]).astype(BF16)
        p_ref[:, sl] = _dot(un, c_ref[...]).astype(BF16)
        q_ref[:, sl] = _dot(un, s_ref[...]).astype(BF16)


def _fnet_channel(proj, fourier_g, cmat, smat, *, tm):
    m = proj.shape[0]
    tm = _fit(tm, m)
    row = pl.BlockSpec((tm, A_WIDTH), lambda i: (i, 0))
    mat = pl.BlockSpec((A_GROUP_DIM, A_GROUP_DIM), lambda i: (0, 0))
    return pl.pallas_call(
        _fnet_channel_body,
        grid=(m // tm,),
        in_specs=[row, pl.BlockSpec((1, A_WIDTH), lambda i: (0, 0)), mat, mat],
        out_specs=[row, row],
        out_shape=[jax.ShapeDtypeStruct((m, A_WIDTH), BF16)] * 2,
        compiler_params=_cparams(1),
        name="fnet_channel",
    )(proj, fourier_g.reshape(1, A_WIDTH), cmat, smat)


def _fnet_seq_body(c_ref, s_ref, p_ref, q_ref, *rest):
    o_ref = rest[-1]
    o_ref[...] = (_dot(c_ref[...], p_ref[...]) + _dot(s_ref[...], q_ref[...])).astype(o_ref.dtype)


def _fnet_seq(cmat, nsmat, p, q, prev, *, row0, nbatch, seq, tm, tn):
    assert row0 % seq == 0
    tm, tn = _fit(tm, seq), _fit(tn, A_WIDTH)
    sb = row0 // seq
    mb = seq // tm
    in_specs = [pl.BlockSpec((tm, seq), lambda m, b, n: (m, 0)),
                pl.BlockSpec((tm, seq), lambda m, b, n: (m, 0)),
                pl.BlockSpec((seq, tn), lambda m, b, n: (sb + b, n)),
                pl.BlockSpec((seq, tn), lambda m, b, n: (sb + b, n))]
    args = [cmat, nsmat, p, q]
    aliases = {}
    if prev is not None:
        in_specs.append(pl.BlockSpec(memory_space=pl.ANY))
        args.append(prev)
        aliases = {4: 0}
    return pl.pallas_call(
        _fnet_seq_body,
        grid=(mb, nbatch, A_WIDTH // tn),
        in_specs=in_specs,
        out_specs=pl.BlockSpec((tm, tn), lambda m, b, n: ((sb + b) * mb + m, n)),
        out_shape=jax.ShapeDtypeStruct((T_ALL, A_WIDTH), BF16),
        input_output_aliases=aliases,
        compiler_params=_cparams(3),
        name=f"fnet_seq_{seq}",
    )(*args)


def _seq_pos(blk, n_prompt_blocks, blocks_per_prompt, blocks_per_sample):
    in_prompt = blk < n_prompt_blocks
    loc = jnp.where(in_prompt, lax.rem(blk, blocks_per_prompt),
                    lax.rem(blk - n_prompt_blocks, blocks_per_sample))
    return loc, jnp.where(in_prompt, blocks_per_prompt, blocks_per_sample)


def _pad_rows(x, before, after):
    parts = [jnp.zeros((n, x.shape[1]), x.dtype) for n in (before,) if n] + [x]
    parts += [jnp.zeros((n, x.shape[1]), x.dtype) for n in (after,) if n]
    return jnp.concatenate(parts, axis=0) if len(parts) > 1 else x


def _gla_body(q_ref, k_ref, v_ref, gt_ref, w2_ref, gb_ref, *rest, reverse, fuse):
    if fuse:
        of_ref, r_ref, hn_ref, o_ref, st_ref = rest
    else:
        o_ref, st_ref = rest
    c = GLA_CHUNK
    n_chunks = T_ALL // c
    step = pl.program_id(0)
    blk = n_chunks - 1 - step if reverse else step
    loc, nloc = _seq_pos(blk, T_PROMPT // c, SEQ // c, DEC_SEQ // c)
    starts_sequence = (loc == nloc - 1) if reverse else (loc == 0)

    @pl.when(starts_sequence)
    def _():
        st_ref[...] = jnp.zeros_like(st_ref)

    z = _dot(gt_ref[...].astype(BF16), w2_ref[...]) + gb_ref[...]
    la = (jnp.minimum(z, 0.0) - jnp.log(1.0 + jnp.exp(-jnp.abs(z)))) * (1.0 / GATE_TEMP)
    row = lax.broadcasted_iota(jnp.int32, (c, c), 0)
    col = lax.broadcasted_iota(jnp.int32, (c, c), 1)
    causal = (col >= row) if reverse else (col <= row)
    tri = jnp.where(causal, 1.0, 0.0).astype(BF16)
    la_hi = la.astype(BF16)
    rem1 = la - la_hi.astype(F32)
    la_mid = rem1.astype(BF16)
    la_lo = (rem1 - la_mid.astype(F32)).astype(BF16)
    g = _dot(tri, la_hi) + _dot(tri, la_mid) + _dot(tri, la_lo)
    edge = 0 if reverse else c - 1
    g_total = g[edge:edge + 1, :]

    n_sub = c // GLA_SUB
    for h in range(B_HEADS):
        ks = slice(h * B_DK, (h + 1) * B_DK)
        vs = slice(h * B_DV, (h + 1) * B_DV)
        q = q_ref[:, ks] * (B_DK ** -0.5)
        k = k_ref[:, ks]
        v = v_ref[:, vs].astype(BF16)
        gh = g[:, ks]
        gt = g_total[:, ks]
        st = st_ref[h]

        o = _dot_nt((q * jnp.exp(gh)).astype(BF16), st.astype(BF16))
        k_out = (k * jnp.exp(gt - gh)).astype(BF16)

        q_parts, k_parts = [], []
        for j in range(n_sub):
            j0 = j * GLA_SUB
            ref_row = j0 + GLA_SUB - 1 if reverse else j0
            gref = gh[ref_row:ref_row + 1, :]
            lo, hi = (0, j0 + GLA_SUB) if reverse else (j0, c)
            qj = q[lo:hi] * jnp.exp(gh[lo:hi] - gref)
            kj = k[j0:j0 + GLA_SUB] * jnp.exp(gref - gh[j0:j0 + GLA_SUB])
            q_parts.append(_pad_rows(qj, lo, c - hi))
            k_parts.append(_pad_rows(kj, j0, c - j0 - GLA_SUB))
        q_cat = jnp.concatenate(q_parts, axis=1).astype(BF16)
        k_cat = jnp.concatenate(k_parts, axis=1).astype(BF16)
        scores = jnp.where(causal, _dot_nt(q_cat, k_cat), 0.0)
        o = o + _dot(scores.astype(BF16), v)

        st_ref[h] = st * jnp.exp(gt) + _dot_tn(v, k_out)

        if fuse:
            o = _standardize(o + of_ref[:, vs]) * hn_ref[:, vs]
            r = r_ref[:, vs]
            o = o * (r * _sigmoid(r))
        o_ref[:, vs] = o.astype(o_ref.dtype)


def _gla(proj, gates, w2, gb, *, reverse, fused_inputs=None):
    c = GLA_CHUNK
    n_chunks = T_ALL // c
    blk = (lambda t: n_chunks - 1 - t) if reverse else (lambda t: t)
    qk_w, v_w = B_QK_WIDTH, B_V_WIDTH
    q_cb = A_WIDTH // qk_w
    v_cb = (A_WIDTH + 2 * qk_w) // v_w
    assert A_WIDTH % qk_w == 0 and (A_WIDTH + 2 * qk_w) % v_w == 0
    in_specs = [pl.BlockSpec((c, qk_w), lambda t: (blk(t), q_cb)),
                pl.BlockSpec((c, qk_w), lambda t: (blk(t), q_cb + 1)),
                pl.BlockSpec((c, v_w), lambda t: (blk(t), v_cb)),
                pl.BlockSpec((c, 2 * GATE_RANK), lambda t: (blk(t), 0)),
                pl.BlockSpec((2 * GATE_RANK, qk_w), lambda t: (0, 0)),
                pl.BlockSpec((1, qk_w), lambda t: (0, 0))]
    args = [proj, proj, proj, gates, w2, gb.reshape(1, qk_w)]
    fuse = fused_inputs is not None
    if fuse:
        o_other, head_norm_g = fused_inputs
        in_specs += [pl.BlockSpec((c, v_w), lambda t: (blk(t), 0)),
                     pl.BlockSpec((c, v_w), lambda t: (blk(t), v_cb + 1)),
                     pl.BlockSpec((1, v_w), lambda t: (0, 0))]
        args += [o_other, proj, head_norm_g.reshape(1, v_w)]
    return pl.pallas_call(
        functools.partial(_gla_body, reverse=reverse, fuse=fuse),
        grid=(n_chunks,),
        in_specs=in_specs,
        out_specs=pl.BlockSpec((c, v_w), lambda t: (blk(t), 0)),
        out_shape=jax.ShapeDtypeStruct((T_ALL, v_w), BF16 if fuse else F32),
        scratch_shapes=[pltpu.VMEM((B_HEADS, B_DV, B_DK), F32)],
        compiler_params=_cparams(1),
        name="gla_bwd_norm_gate" if fuse else "gla_fwd",
    )(*args)


def _t5_bucket(rel):
    nb = N_BUCKETS // 2
    max_exact = nb // 2
    ret = jnp.where(rel > 0, nb, 0)
    n = jnp.abs(rel)
    nf = jnp.maximum(n, 1).astype(jnp.float32)
    large = max_exact + (jnp.log(nf / max_exact) / math.log(MAX_DISTANCE / max_exact)
                         * (nb - max_exact)).astype(jnp.int32)
    large = jnp.minimum(large, nb - 1)
    return ret + jnp.where(n < max_exact, n, large)


def _bias_body(table_ref, bucket_ref, o_ref):
    h = pl.program_id(0)
    blk, kw = bucket_ref.shape
    bucket = bucket_ref[...]
    qq = lax.broadcasted_iota(jnp.int32, (blk, kw), 0)
    kk = lax.broadcasted_iota(jnp.int32, (blk, kw), 1)
    rel = kk - ATTN_BLOCK - qq
    acc = jnp.zeros((blk, kw), F32)
    for b in range(N_BUCKETS):
        acc = jnp.where(bucket == b, table_ref[b, h], acc)
    o_ref[0] = jnp.where(jnp.abs(rel) <= WINDOW, acc, NEG_BIG)


def _pair_of_head(h):
    kv = h // C_GROUP
    return (kv // 2) * C_GROUP + h % C_GROUP, kv % 2


def _band_bias(table):
    kw = 3 * ATTN_BLOCK
    rel = jnp.arange(kw)[None, :] - ATTN_BLOCK - jnp.arange(ATTN_BLOCK)[:, None]
    bucket = _t5_bucket(rel).astype(jnp.int32)

    def out_index(h):
        pair, slot = _pair_of_head(h)
        return pair, 0, slot

    return pl.pallas_call(
        _bias_body,
        grid=(C_HEADS,),
        in_specs=[pl.BlockSpec(memory_space=pltpu.SMEM),
                  pl.BlockSpec((ATTN_BLOCK, kw), lambda h: (0, 0))],
        out_specs=pl.BlockSpec((1, ATTN_BLOCK, kw), out_index),
        out_shape=jax.ShapeDtypeStruct((C_HEADS // 2, ATTN_BLOCK, 2 * kw), F32),
        compiler_params=_cparams(1),
        name="band_bias",
    )(table.astype(F32), bucket)


def _attn_body(q_ref, kl_ref, kc_ref, kr_ref, vl_ref, vc_ref, vr_ref, bias_ref, sink_ref, o_ref):
    blk = ATTN_BLOCK
    kw = 3 * blk
    n = pl.program_id(0)
    loc, nloc = _seq_pos(n, T_PROMPT // blk, SEQ // blk, DEC_SEQ // blk)
    lane = lax.broadcasted_iota(jnp.int32, (1, 2 * kw), 1)
    kk = jnp.where(lane >= kw, lane - kw, lane)
    outside = ((loc == 0) & (kk < blk)) | ((loc == nloc - 1) & (kk >= 2 * blk))
    penalty = jnp.where(outside, NEG_BIG, 0.0)

    lane_d = lax.broadcasted_iota(jnp.int32, (kw, LANES), 1)
    is_a = lane_d < C_HEAD_DIM
    lane_o = lax.broadcasted_iota(jnp.int32, (blk, LANES), 1) < C_HEAD_DIM
    for m in range(C_KV_HEADS // 2):
        sl = slice(m * LANES, (m + 1) * LANES)
        k3 = jnp.concatenate([kl_ref[:, sl], kc_ref[:, sl], kr_ref[:, sl]], axis=0)
        v3 = jnp.concatenate([vl_ref[:, sl], vc_ref[:, sl], vr_ref[:, sl]], axis=0)
        zero = jnp.zeros_like(k3)
        kb = jnp.concatenate([jnp.where(is_a, k3, zero), jnp.where(is_a, zero, k3)], axis=0)
        vb = jnp.concatenate([jnp.where(is_a, v3, zero), jnp.where(is_a, zero, v3)], axis=0)
        for gi in range(C_GROUP):
            pair = m * C_GROUP + gi
            ps = slice(pair * LANES, (pair + 1) * LANES)
            qp = q_ref[:, ps] * (C_HEAD_DIM ** -0.5)
            s = _dot_nt(qp, kb) + bias_ref[pair] + penalty
            sink_a = sink_ref[2 * m * C_GROUP + gi]
            sink_b = sink_ref[(2 * m + 1) * C_GROUP + gi]
            s_a, s_b = s[:, :kw], s[:, kw:]
            m_a = jnp.maximum(jnp.max(s_a, axis=-1, keepdims=True), sink_a)
            m_b = jnp.maximum(jnp.max(s_b, axis=-1, keepdims=True), sink_b)
            p_a = jnp.exp(s_a - m_a)
            p_b = jnp.exp(s_b - m_b)
            d_a = jnp.sum(p_a, axis=-1, keepdims=True) + jnp.exp(sink_a - m_a)
            d_b = jnp.sum(p_b, axis=-1, keepdims=True) + jnp.exp(sink_b - m_b)
            pv = _dot(jnp.concatenate([p_a, p_b], axis=1).astype(BF16), vb)
            o_ref[:, ps] = (pv / jnp.where(lane_o, d_a, d_b)).astype(o_ref.dtype)


def _window_attention(proj, bias, sinks):
    blk = ATTN_BLOCK
    nblk = T_ALL // blk
    k_cb = C_Q_WIDTH // C_KV_WIDTH
    assert C_Q_WIDTH % C_KV_WIDTH == 0
    left = lambda n: jnp.maximum(n - 1, 0)
    right = lambda n: jnp.minimum(n + 1, nblk - 1)

    def kv_specs(cb):
        return [pl.BlockSpec((blk, C_KV_WIDTH), lambda n: (left(n), cb)),
                pl.BlockSpec((blk, C_KV_WIDTH), lambda n: (n, cb)),
                pl.BlockSpec((blk, C_KV_WIDTH), lambda n: (right(n), cb))]

    return pl.pallas_call(
        _attn_body,
        grid=(nblk,),
        in_specs=[pl.BlockSpec((blk, C_Q_WIDTH), lambda n: (n, 0))] + kv_specs(k_cb) + kv_specs(k_cb + 1)
        + [pl.BlockSpec((C_HEADS // 2, blk, 6 * blk), lambda n: (0, 0, 0)),
           pl.BlockSpec(memory_space=pltpu.SMEM)],
        out_specs=pl.BlockSpec((blk, C_Q_WIDTH), lambda n: (n, 0)),
        out_shape=jax.ShapeDtypeStruct((T_ALL, C_Q_WIDTH), BF16),
        compiler_params=_cparams(1),
        name="window_attention",
    )(proj, proj, proj, proj, proj, proj, proj, bias, sinks.astype(F32))


def _pair_head_order():
    order = [0] * C_HEADS
    for h in range(C_HEADS):
        pair, slot = _pair_of_head(h)
        order[2 * pair + slot] = h
    return jnp.array(order, dtype=jnp.int32)


def _mixer_ab(xb, w_in, fourier_g, gate_w2, gate_b, head_norm_g, w_out, dft):
    w_in_b = w_in.astype(BF16)
    proj = _matmul(xb, w_in_b, AB_MAIN, tm=1024, tn=1024, out_dtype=F32, name="ab_proj")
    gates = _matmul(xb, w_in_b[:, AB_MAIN:], 2 * GATE_RANK, tm=1024, tn=2 * GATE_RANK,
                    out_dtype=F32, name="ab_gate_proj")

    chan_c, chan_s, seq_mats = dft
    p, q = _fnet_channel(proj, fourier_g, chan_c, chan_s, tm=512)
    a_out = None
    for row0, nbatch, seq in ((0, BATCH, SEQ), (T_PROMPT, DEC_BATCH, DEC_SEQ)):
        cmat, nsmat = seq_mats[seq]
        a_out = _fnet_seq(cmat, nsmat, p, q, a_out, row0=row0, nbatch=nbatch, seq=seq,
                          tm=min(seq, 512), tn=512)

    zeros = jnp.zeros((GATE_RANK, B_QK_WIDTH), F32)
    w2_f = jnp.concatenate([gate_w2[0].astype(F32), zeros], axis=0).astype(BF16)
    w2_b = jnp.concatenate([zeros, gate_w2[1].astype(F32)], axis=0).astype(BF16)
    o_f = _gla(proj, gates, w2_f, gate_b[0].astype(F32), reverse=False)
    b_out = _gla(proj, gates, w2_b, gate_b[1].astype(F32), reverse=True,
                 fused_inputs=(o_f, head_norm_g.astype(F32)))

    return _matmul_split_lhs(a_out, b_out, w_out.astype(BF16), tm=1024, tn=1024, name="ab_out_proj")


def _mixer_c(xb, w_in, sinks, w_out, bias):
    order = _pair_head_order()
    wq = w_in[:, :C_Q_WIDTH].reshape(D_MODEL, C_HEADS, C_HEAD_DIM)[:, order, :].reshape(D_MODEL, C_Q_WIDTH)
    w_in_p = jnp.concatenate([wq, w_in[:, C_Q_WIDTH:]], axis=1).astype(BF16)
    wo = w_out.reshape(C_HEADS, C_HEAD_DIM, D_MODEL)[order].reshape(C_Q_WIDTH, D_MODEL).astype(BF16)
    n_cols = C_Q_WIDTH + 2 * C_KV_WIDTH
    proj = _matmul(xb, w_in_p, n_cols, tm=1024, tn=1024, out_dtype=BF16, name="c_proj")
    o = _window_attention(proj, bias, sinks)
    return _matmul(o, wo, D_MODEL, tm=1024, tn=1024, out_dtype=F32, name="c_out_proj")


def kernel(x_prompt, x_sample, rel_bias_table, ab_w_in, ab_fourier_g, ab_gate_w2, ab_gate_b, ab_head_norm_g,
           ab_w_out, c_w_in, c_sinks, c_w_out, ffn_w1, ffn_w3, ffn_w2, ln_g, ln_b):
    x = jnp.concatenate([x_prompt.reshape(T_PROMPT, D_MODEL), x_sample.reshape(T_SAMPLE, D_MODEL)], axis=0)
    x = x.astype(F32)
    xb = x.astype(BF16)

    bias = _band_bias(rel_bias_table)
    chan_c, chan_s = _dft_matrices(A_GROUP_DIM, sin_sign=1.0, tr=min(A_GROUP_DIM, 256))
    seq_mats = {s: _dft_matrices(s, sin_sign=-1.0, tr=min(s, 256)) for s in sorted({SEQ, DEC_SEQ})}
    dft = (chan_c, chan_s, seq_mats)

    for i in range(DEPTH):
        j = i // 2
        if i % 2 == 0:
            h = _mixer_ab(xb, ab_w_in[j], ab_fourier_g[j], ab_gate_w2[j], ab_gate_b[j],
                          ab_head_norm_g[j], ab_w_out[j], dft)
        else:
            h = _mixer_c(xb, c_w_in[j], c_sinks[j], c_w_out[j], bias)
        x, xb = _deepnorm(x, h, ln_g[i, 0].astype(F32), ln_b[i, 0].astype(F32), tm=256)
        hidden = _swiglu_up(xb, ffn_w1[i].astype(BF16), ffn_w3[i].astype(BF16), tm=2048, tn=256)
        f = _matmul(hidden, ffn_w2[i].astype(BF16), D_MODEL, tm=512, tn=512, out_dtype=F32, name="ffn_down")
        x, xb = _deepnorm(x, f, ln_g[i, 1].astype(F32), ln_b[i, 1].astype(F32), tm=256)

    y_prompt = x[:T_PROMPT].reshape(BATCH, SEQ, D_MODEL)
    y_sample = x[T_PROMPT:].reshape(DEC_BATCH, DEC_SEQ, D_MODEL)
    return (y_prompt, y_sample)
```

```python
import functools
import math

import jax
import jax.numpy as jnp
from jax import lax
from jax.experimental import pallas as pl
from jax.experimental.pallas import tpu as pltpu

D_MODEL = 4096
BATCH = 8
SEQ = 2048
DEPTH = 2
DEC_BATCH = 4
DEC_SEQ = 4096

A_GROUPS = 4
A_GROUP_DIM = D_MODEL // 8
A_WIDTH = A_GROUPS * A_GROUP_DIM
B_HEADS = 4
B_DV = D_MODEL // 8
B_DK = B_DV // 2
B_QK_WIDTH = B_HEADS * B_DK
B_V_WIDTH = B_HEADS * B_DV
GATE_RANK = 16
GATE_TEMP = 16.0
AB_MAIN = A_WIDTH + 2 * B_QK_WIDTH + 2 * B_V_WIDTH
C_HEAD_DIM = 64
C_HEADS = D_MODEL // C_HEAD_DIM
C_KV_HEADS = 8
C_GROUP = C_HEADS // C_KV_HEADS
C_Q_WIDTH = C_HEADS * C_HEAD_DIM
C_KV_WIDTH = C_KV_HEADS * C_HEAD_DIM
WINDOW = 128
ATTN_BLOCK = 128
N_BUCKETS = 32
MAX_DISTANCE = 128
D_FF = -(-8 * D_MODEL // (3 * 256)) * 256
DEEPNORM_ALPHA = (2 * DEPTH) ** 0.25
LN_EPS = 1e-5

T_PROMPT = BATCH * SEQ
T_SAMPLE = DEC_BATCH * DEC_SEQ
T_ALL = T_PROMPT + T_SAMPLE

F32 = jnp.float32
BF16 = jnp.bfloat16
NEG_BIG = -1e30

V7X_VMEM_LIMIT_BYTES = 56 * 1024 * 1024
LANES = 128
GLA_CHUNK = 128
GLA_SUB = 16


def _cparams(n_axes):
    return pltpu.CompilerParams(dimension_semantics=("arbitrary",) * n_axes,
                                vmem_limit_bytes=V7X_VMEM_LIMIT_BYTES)


def _fit(tile, dim):
    tile = min(tile, dim)
    assert dim % tile == 0, (tile, dim)
    return tile


def _dot(a, b):
    return jnp.dot(a, b, preferred_element_type=F32)


def _dot_nt(a, b):
    return lax.dot_general(a, b, (((1,), (1,)), ((), ())), preferred_element_type=F32)


def _dot_tn(a, b):
    return lax.dot_general(a, b, (((0,), (0,)), ((), ())), preferred_element_type=F32)


def _standardize(v):
    mu = jnp.mean(v, axis=-1, keepdims=True)
    c = v - mu
    return c * lax.rsqrt(jnp.mean(c * c, axis=-1, keepdims=True) + LN_EPS)


def _sigmoid(v):
    return 1.0 / (1.0 + jnp.exp(-v))


def _mm_body(x_ref, w_ref, o_ref):
    o_ref[...] = _dot(x_ref[...], w_ref[...]).astype(o_ref.dtype)


def _matmul(x, w, n_cols, *, tm, tn, out_dtype, name, col0=0):
    m, k = x.shape
    tm, tn = _fit(tm, m), _fit(tn, n_cols)
    cb = col0 // tn
    return pl.pallas_call(
        _mm_body,
        grid=(m // tm, n_cols // tn),
        in_specs=[pl.BlockSpec((tm, k), lambda i, j: (i, 0)),
                  pl.BlockSpec((k, tn), lambda i, j: (0, j + cb))],
        out_specs=pl.BlockSpec((tm, tn), lambda i, j: (i, j)),
        out_shape=jax.ShapeDtypeStruct((m, n_cols), out_dtype),
        compiler_params=_cparams(2),
        name=name,
    )(x, w)


def _mm2_body(a_ref, b_ref, w_ref, o_ref):
    ka = a_ref.shape[1]
    acc = _dot(a_ref[...], w_ref[:ka, :]) + _dot(b_ref[...], w_ref[ka:, :])
    o_ref[...] = acc.astype(o_ref.dtype)


def _matmul_split_lhs(a, b, w, *, tm, tn, name):
    m, ka = a.shape
    kb = b.shape[1]
    n = w.shape[1]
    tm, tn = _fit(tm, m), _fit(tn, n)
    return pl.pallas_call(
        _mm2_body,
        grid=(m // tm, n // tn),
        in_specs=[pl.BlockSpec((tm, ka), lambda i, j: (i, 0)),
                  pl.BlockSpec((tm, kb), lambda i, j: (i, 0)),
                  pl.BlockSpec((ka + kb, tn), lambda i, j: (0, j))],
        out_specs=pl.BlockSpec((tm, tn), lambda i, j: (i, j)),
        out_shape=jax.ShapeDtypeStruct((m, n), F32),
        compiler_params=_cparams(2),
        name=name,
    )(a, b, w)


def _tile_cols(w, tn):
    layers, k, n = w.shape
    return w.reshape(layers, k, n // tn, tn).transpose(0, 2, 1, 3)


def _swiglu_body(x_ref, w1_ref, w3_ref, o_ref, *, sub):
    w1 = w1_ref[...].astype(BF16)
    w3 = w3_ref[...].astype(BF16)
    for r in range(x_ref.shape[0] // sub):
        rows = slice(r * sub, (r + 1) * sub)
        x = x_ref[rows, :]
        a = _dot(x, w1)
        b = _dot(x, w3)
        o_ref[rows, :] = (a * _sigmoid(a) * b).astype(o_ref.dtype)


def _swiglu_up(x, w1, w3, layer, *, tm, tn, sub):
    m, k = x.shape
    n = w1.shape[2]
    tm, tn = _fit(tm, m), _fit(tn, n)
    wspec = pl.BlockSpec((None, k, tn), lambda i, j: (layer, 0, j))
    return pl.pallas_call(
        functools.partial(_swiglu_body, sub=_fit(sub, tm)),
        grid=(m // tm, n // tn),
        in_specs=[pl.BlockSpec((tm, k), lambda i, j: (i, 0)), wspec, wspec],
        out_specs=pl.BlockSpec((tm, tn), lambda i, j: (i, j)),
        out_shape=jax.ShapeDtypeStruct((m, n), BF16),
        compiler_params=_cparams(2),
        name="swiglu_up",
    )(x, w1, w3)


def _ffn_down(x, w2t, layer, *, tm):
    m, k = x.shape
    _, n_tiles, _, tn = w2t.shape
    tm = _fit(tm, m)
    return pl.pallas_call(
        _mm_body,
        grid=(m // tm, n_tiles),
        in_specs=[pl.BlockSpec((tm, k), lambda i, j: (i, 0)),
                  pl.BlockSpec((None, None, k, tn), lambda i, j: (layer, j, 0, 0))],
        out_specs=pl.BlockSpec((tm, tn), lambda i, j: (i, j)),
        out_shape=jax.ShapeDtypeStruct((m, n_tiles * tn), F32),
        compiler_params=_cparams(2),
        name="ffn_down",
    )(x, w2t)


def _first_rows(n_first):
    return (lambda i: (jnp.minimum(i, n_first - 1), 0)), (lambda i: (jnp.maximum(i - n_first, 0), 0))


def _cast_body(xp_ref, xs_ref, o_ref, *, n_first):
    i = pl.program_id(0)

    @pl.when(i < n_first)
    def _():
        o_ref[...] = xp_ref[...].astype(o_ref.dtype)

    @pl.when(i >= n_first)
    def _():
        o_ref[...] = xs_ref[...].astype(o_ref.dtype)


def _concat_cast(x_first, x_rest, *, tm):
    (m1, d), m2 = x_first.shape, x_rest.shape[0]
    tm = _fit(_fit(tm, m1), m2)
    first, rest = _first_rows(m1 // tm)
    return pl.pallas_call(
        functools.partial(_cast_body, n_first=m1 // tm),
        grid=((m1 + m2) // tm,),
        in_specs=[pl.BlockSpec((tm, d), first), pl.BlockSpec((tm, d), rest)],
        out_specs=pl.BlockSpec((tm, d), lambda i: (i, 0)),
        out_shape=jax.ShapeDtypeStruct((m1 + m2, d), BF16),
        compiler_params=_cparams(1),
        name="concat_cast",
    )(x_first, x_rest)


def _ln_body(*refs, n_first, two_in, two_out, with_bf16):
    n_x = 2 if two_in else 1
    x_refs, (h_ref, g_ref, b_ref), out_refs = refs[:n_x], refs[n_x:n_x + 3], refs[n_x + 3:]

    def run(x_ref, y_ref):
        y = _standardize(DEEPNORM_ALPHA * x_ref[...] + h_ref[...]) * g_ref[...] + b_ref[...]
        y_ref[...] = y
        if with_bf16:
            out_refs[-1][...] = y.astype(BF16)

    if not (two_in or two_out):
        run(x_refs[0], out_refs[0])
        return
    i = pl.program_id(0)
    pl.when(i < n_first)(lambda: run(x_refs[0], out_refs[0]))
    pl.when(i >= n_first)(lambda: run(x_refs[-1], out_refs[1 if two_out else 0]))


def _deepnorm(xs, h, g, b, *, tm, split_out=False, with_bf16=True):
    m, d = h.shape
    tm = _fit(_fit(tm, T_PROMPT), T_SAMPLE)
    first, rest = _first_rows(T_PROMPT // tm)
    row = pl.BlockSpec((tm, d), lambda i: (i, 0))
    vec = pl.BlockSpec((1, d), lambda i: (0, 0))
    two_in = len(xs) == 2
    x_specs = [pl.BlockSpec((tm, d), first), pl.BlockSpec((tm, d), rest)] if two_in else [row]
    if split_out:
        out_specs = [pl.BlockSpec((tm, d), first), pl.BlockSpec((tm, d), rest)]
        out_shape = [jax.ShapeDtypeStruct((T_PROMPT, d), F32), jax.ShapeDtypeStruct((T_SAMPLE, d), F32)]
    else:
        out_specs, out_shape = [row], [jax.ShapeDtypeStruct((m, d), F32)]
    if with_bf16:
        out_specs, out_shape = out_specs + [row], out_shape + [jax.ShapeDtypeStruct((m, d), BF16)]
    return pl.pallas_call(
        functools.partial(_ln_body, n_first=T_PROMPT // tm, two_in=two_in, two_out=split_out,
                          with_bf16=with_bf16),
        grid=(m // tm,),
        in_specs=x_specs + [row, vec, vec],
        out_specs=out_specs,
        out_shape=out_shape,
        compiler_params=_cparams(1),
        name="deepnorm",
    )(*xs, h, g.reshape(1, d), b.reshape(1, d))


def _dft_body(c_ref, s_ref, ci_ref, si_ref, *, n, scale, sin_sign):
    tr, nc = c_ref.shape
    i = pl.program_id(0)
    unit = 2.0 * math.pi / n

    @pl.when(i == 0)
    def _():
        row = lax.broadcasted_iota(jnp.int32, (tr, nc), 0)
        col = lax.broadcasted_iota(jnp.int32, (tr, nc), 1)
        ang = ((row * col) & (n - 1)).astype(F32) * unit
        ci_ref[...] = jnp.cos(ang)
        si_ref[...] = jnp.sin(ang)

    col = lax.broadcasted_iota(jnp.int32, (1, nc), 1)
    ang0 = (((i * tr) * col) & (n - 1)).astype(F32) * unit
    c0 = jnp.cos(ang0) * scale
    s0 = jnp.sin(ang0) * scale
    ci = ci_ref[...]
    si = si_ref[...]
    c_ref[...] = (c0 * ci - s0 * si).astype(BF16)
    s_ref[...] = ((s0 * ci + c0 * si) * sin_sign).astype(BF16)


def _dft_matrices(n, *, sin_sign, tr):
    assert n & (n - 1) == 0
    tr = _fit(tr, n)
    blk = pl.BlockSpec((tr, n), lambda i: (i, 0))
    return pl.pallas_call(
        functools.partial(_dft_body, n=n, scale=n ** -0.5, sin_sign=sin_sign),
        grid=(n // tr,),
        out_specs=[blk, blk],
        out_shape=[jax.ShapeDtypeStruct((n, n), BF16)] * 2,
        scratch_shapes=[pltpu.VMEM((tr, n), F32)] * 2,
        compiler_params=_cparams(1),
        name=f"dft_matrices_{n}",
    )()


def _fnet_channel_body(u_ref, g_ref, c_ref, s_ref, p_ref, q_ref):
    for grp in range(A_GROUPS):
        sl = slice(grp * A_GROUP_DIM, (grp + 1) * A_GROUP_DIM)
        un = (_standardize(u_ref[:, sl]) * g_ref[:, sl]).astype(BF16)
        p_ref[:, sl] = _dot(un, c_ref[...]).astype(BF16)
        q_ref[:, sl] = _dot(un, s_ref[...]).astype(BF16)


def _fnet_channel(proj, fourier_g, cmat, smat, *, tm):
    m = proj.shape[0]
    tm = _fit(tm, m)
    row = pl.BlockSpec((tm, A_WIDTH), lambda i: (i, 0))
    mat = pl.BlockSpec((A_GROUP_DIM, A_GROUP_DIM), lambda i: (0, 0))
    return pl.pallas_call(
        _fnet_channel_body,
        grid=(m // tm,),
        in_specs=[row, pl.BlockSpec((1, A_WIDTH), lambda i: (0, 0)), mat, mat],
        out_specs=[row, row],
        out_shape=[jax.ShapeDtypeStruct((m, A_WIDTH), BF16)] * 2,
        compiler_params=_cparams(1),
        name="fnet_channel",
    )(proj, fourier_g.reshape(1, A_WIDTH), cmat, smat)


def _fnet_seq_body(c_ref, s_ref, p_ref, q_ref, *rest):
    o_ref = rest[-1]
    o_ref[...] = (_dot(c_ref[...], p_ref[...]) + _dot(s_ref[...], q_ref[...])).astype(o_ref.dtype)


def _fnet_seq(cmat, nsmat, p, q, prev, *, row0, nbatch, seq, tm, tn):
    assert row0 % seq == 0
    tm, tn = _fit(tm, seq), _fit(tn, A_WIDTH)
    sb = row0 // seq
    mb = seq // tm
    in_specs = [pl.BlockSpec((tm, seq), lambda m, b, n: (m, 0)),
                pl.BlockSpec((tm, seq), lambda m, b, n: (m, 0)),
                pl.BlockSpec((seq, tn), lambda m, b, n: (sb + b, n)),
                pl.BlockSpec((seq, tn), lambda m, b, n: (sb + b, n))]
    args = [cmat, nsmat, p, q]
    aliases = {}
    if prev is not None:
        in_specs.append(pl.BlockSpec(memory_space=pl.ANY))
        args.append(prev)
        aliases = {4: 0}
    return pl.pallas_call(
        _fnet_seq_body,
        grid=(mb, nbatch, A_WIDTH // tn),
        in_specs=in_specs,
        out_specs=pl.BlockSpec((tm, tn), lambda m, b, n: ((sb + b) * mb + m, n)),
        out_shape=jax.ShapeDtypeStruct((T_ALL, A_WIDTH), BF16),
        input_output_aliases=aliases,
        compiler_params=_cparams(3),
        name=f"fnet_seq_{seq}",
    )(*args)


def _seq_pos(blk, n_prompt_blocks, blocks_per_prompt, blocks_per_sample):
    in_prompt = blk < n_prompt_blocks
    loc = jnp.where(in_prompt, lax.rem(blk, blocks_per_prompt),
                    lax.rem(blk - n_prompt_blocks, blocks_per_sample))
    return loc, jnp.where(in_prompt, blocks_per_prompt, blocks_per_sample)


def _pad_rows(x, before, after):
    parts = [jnp.zeros((n, x.shape[1]), x.dtype) for n in (before,) if n] + [x]
    parts += [jnp.zeros((n, x.shape[1]), x.dtype) for n in (after,) if n]
    return jnp.concatenate(parts, axis=0) if len(parts) > 1 else x


def _gla_body(q_ref, k_ref, v_ref, gt_ref, w2_ref, gb_ref, *rest, reverse, fuse):
    if fuse:
        of_ref, r_ref, hn_ref, o_ref, st_ref = rest
    else:
        o_ref, st_ref = rest
    c = GLA_CHUNK
    n_chunks = T_ALL // c
    step = pl.program_id(0)
    blk = n_chunks - 1 - step if reverse else step
    loc, nloc = _seq_pos(blk, T_PROMPT // c, SEQ // c, DEC_SEQ // c)
    starts_sequence = (loc == nloc - 1) if reverse else (loc == 0)

    @pl.when(starts_sequence)
    def _():
        st_ref[...] = jnp.zeros_like(st_ref)

    z = _dot(gt_ref[...].astype(BF16), w2_ref[...]) + gb_ref[...]
    la = (jnp.minimum(z, 0.0) - jnp.log(1.0 + jnp.exp(-jnp.abs(z)))) * (1.0 / GATE_TEMP)
    row = lax.broadcasted_iota(jnp.int32, (c, c), 0)
    col = lax.broadcasted_iota(jnp.int32, (c, c), 1)
    causal = (col >= row) if reverse else (col <= row)
    tri = jnp.where(causal, 1.0, 0.0).astype(BF16)
    la_hi = la.astype(BF16)
    rem1 = la - la_hi.astype(F32)
    la_mid = rem1.astype(BF16)
    la_lo = (rem1 - la_mid.astype(F32)).astype(BF16)
    g = _dot(tri, la_hi) + _dot(tri, la_mid) + _dot(tri, la_lo)
    edge = 0 if reverse else c - 1
    g_total = g[edge:edge + 1, :]

    n_sub = c // GLA_SUB
    for h in range(B_HEADS):
        ks = slice(h * B_DK, (h + 1) * B_DK)
        vs = slice(h * B_DV, (h + 1) * B_DV)
        q = q_ref[:, ks] * (B_DK ** -0.5)
        k = k_ref[:, ks]
        v = v_ref[:, vs].astype(BF16)
        gh = g[:, ks]
        gt = g_total[:, ks]
        st = st_ref[h]

        o = _dot_nt((q * jnp.exp(gh)).astype(BF16), st.astype(BF16))
        k_out = (k * jnp.exp(gt - gh)).astype(BF16)

        q_parts, k_parts = [], []
        for j in range(n_sub):
            j0 = j * GLA_SUB
            ref_row = j0 + GLA_SUB - 1 if reverse else j0
            gref = gh[ref_row:ref_row + 1, :]
            lo, hi = (0, j0 + GLA_SUB) if reverse else (j0, c)
            qj = q[lo:hi] * jnp.exp(gh[lo:hi] - gref)
            kj = k[j0:j0 + GLA_SUB] * jnp.exp(gref - gh[j0:j0 + GLA_SUB])
            q_parts.append(_pad_rows(qj, lo, c - hi))
            k_parts.append(_pad_rows(kj, j0, c - j0 - GLA_SUB))
        q_cat = jnp.concatenate(q_parts, axis=1).astype(BF16)
        k_cat = jnp.concatenate(k_parts, axis=1).astype(BF16)
        scores = jnp.where(causal, _dot_nt(q_cat, k_cat), 0.0)
        o = o + _dot(scores.astype(BF16), v)

        st_ref[h] = st * jnp.exp(gt) + _dot_tn(v, k_out)

        if fuse:
            o = _standardize(o + of_ref[:, vs]) * hn_ref[:, vs]
            r = r_ref[:, vs]
            o = o * (r * _sigmoid(r))
        o_ref[:, vs] = o.astype(o_ref.dtype)


def _gla(proj, gates, w2, gb, *, reverse, fused_inputs=None):
    c = GLA_CHUNK
    n_chunks = T_ALL // c
    blk = (lambda t: n_chunks - 1 - t) if reverse else (lambda t: t)
    qk_w, v_w = B_QK_WIDTH, B_V_WIDTH
    q_cb = A_WIDTH // qk_w
    v_cb = (A_WIDTH + 2 * qk_w) // v_w
    assert A_WIDTH % qk_w == 0 and (A_WIDTH + 2 * qk_w) % v_w == 0
    in_specs = [pl.BlockSpec((c, qk_w), lambda t: (blk(t), q_cb)),
                pl.BlockSpec((c, qk_w), lambda t: (blk(t), q_cb + 1)),
                pl.BlockSpec((c, v_w), lambda t: (blk(t), v_cb)),
                pl.BlockSpec((c, 2 * GATE_RANK), lambda t: (blk(t), 0)),
                pl.BlockSpec((2 * GATE_RANK, qk_w), lambda t: (0, 0)),
                pl.BlockSpec((1, qk_w), lambda t: (0, 0))]
    args = [proj, proj, proj, gates, w2, gb.reshape(1, qk_w)]
    fuse = fused_inputs is not None
    if fuse:
        o_other, head_norm_g = fused_inputs
        in_specs += [pl.BlockSpec((c, v_w), lambda t: (blk(t), 0)),
                     pl.BlockSpec((c, v_w), lambda t: (blk(t), v_cb + 1)),
                     pl.BlockSpec((1, v_w), lambda t: (0, 0))]
        args += [o_other, proj, head_norm_g.reshape(1, v_w)]
    return pl.pallas_call(
        functools.partial(_gla_body, reverse=reverse, fuse=fuse),
        grid=(n_chunks,),
        in_specs=in_specs,
        out_specs=pl.BlockSpec((c, v_w), lambda t: (blk(t), 0)),
        out_shape=jax.ShapeDtypeStruct((T_ALL, v_w), BF16 if fuse else F32),
        scratch_shapes=[pltpu.VMEM((B_HEADS, B_DV, B_DK), F32)],
        compiler_params=_cparams(1),
        name="gla_bwd_norm_gate" if fuse else "gla_fwd",
    )(*args)


def _t5_bucket(rel):
    nb = N_BUCKETS // 2
    max_exact = nb // 2
    ret = jnp.where(rel > 0, nb, 0)
    n = jnp.abs(rel)
    nf = jnp.maximum(n, 1).astype(jnp.float32)
    large = max_exact + (jnp.log(nf / max_exact) / math.log(MAX_DISTANCE / max_exact)
                         * (nb - max_exact)).astype(jnp.int32)
    large = jnp.minimum(large, nb - 1)
    return ret + jnp.where(n < max_exact, n, large)


def _bias_body(table_ref, bucket_ref, o_ref):
    h = pl.program_id(0)
    blk, kw = bucket_ref.shape
    bucket = bucket_ref[...]
    qq = lax.broadcasted_iota(jnp.int32, (blk, kw), 0)
    kk = lax.broadcasted_iota(jnp.int32, (blk, kw), 1)
    rel = kk - ATTN_BLOCK - qq
    acc = jnp.zeros((blk, kw), F32)
    for b in range(N_BUCKETS):
        acc = jnp.where(bucket == b, table_ref[b, h], acc)
    o_ref[0] = jnp.where(jnp.abs(rel) <= WINDOW, acc, NEG_BIG)


def _pair_of_head(h):
    kv = h // C_GROUP
    return (kv // 2) * C_GROUP + h % C_GROUP, kv % 2


def _band_bias(table):
    kw = 3 * ATTN_BLOCK
    rel = jnp.arange(kw)[None, :] - ATTN_BLOCK - jnp.arange(ATTN_BLOCK)[:, None]
    bucket = _t5_bucket(rel).astype(jnp.int32)

    def out_index(h):
        pair, slot = _pair_of_head(h)
        return pair, 0, slot

    return pl.pallas_call(
        _bias_body,
        grid=(C_HEADS,),
        in_specs=[pl.BlockSpec(memory_space=pltpu.SMEM),
                  pl.BlockSpec((ATTN_BLOCK, kw), lambda h: (0, 0))],
        out_specs=pl.BlockSpec((1, ATTN_BLOCK, kw), out_index),
        out_shape=jax.ShapeDtypeStruct((C_HEADS // 2, ATTN_BLOCK, 2 * kw), F32),
        compiler_params=_cparams(1),
        name="band_bias",
    )(table.astype(F32), bucket)


def _attn_body(q_ref, kl_ref, kc_ref, kr_ref, vl_ref, vc_ref, vr_ref, bias_ref, sink_ref, o_ref):
    blk = ATTN_BLOCK
    kw = 3 * blk
    n = pl.program_id(0)
    loc, nloc = _seq_pos(n, T_PROMPT // blk, SEQ // blk, DEC_SEQ // blk)
    lane = lax.broadcasted_iota(jnp.int32, (1, 2 * kw), 1)
    kk = jnp.where(lane >= kw, lane - kw, lane)
    outside = ((loc == 0) & (kk < blk)) | ((loc == nloc - 1) & (kk >= 2 * blk))
    penalty = jnp.where(outside, NEG_BIG, 0.0)

    lane_d = lax.broadcasted_iota(jnp.int32, (kw, LANES), 1)
    is_a = lane_d < C_HEAD_DIM
    lane_o = lax.broadcasted_iota(jnp.int32, (blk, LANES), 1) < C_HEAD_DIM
    for m in range(C_KV_HEADS // 2):
        sl = slice(m * LANES, (m + 1) * LANES)
        k3 = jnp.concatenate([kl_ref[:, sl], kc_ref[:, sl], kr_ref[:, sl]], axis=0)
        v3 = jnp.concatenate([vl_ref[:, sl], vc_ref[:, sl], vr_ref[:, sl]], axis=0)
        zero = jnp.zeros_like(k3)
        kb = jnp.concatenate([jnp.where(is_a, k3, zero), jnp.where(is_a, zero, k3)], axis=0)
        vb = jnp.concatenate([jnp.where(is_a, v3, zero), jnp.where(is_a, zero, v3)], axis=0)
        pairs = [m * C_GROUP + gi for gi in range(C_GROUP)]
        q_all = jnp.concatenate([q_ref[:, p * LANES:(p + 1) * LANES] for p in pairs], axis=0)
        s_all = _dot_nt(q_all * (C_HEAD_DIM ** -0.5), kb)
        probs, denoms = [], []
        for gi, pair in enumerate(pairs):
            s = s_all[gi * blk:(gi + 1) * blk] + bias_ref[pair] + penalty
            sink_a = sink_ref[2 * m * C_GROUP + gi]
            sink_b = sink_ref[(2 * m + 1) * C_GROUP + gi]
            s_a, s_b = s[:, :kw], s[:, kw:]
            m_a = jnp.maximum(jnp.max(s_a, axis=-1, keepdims=True), sink_a)
            m_b = jnp.maximum(jnp.max(s_b, axis=-1, keepdims=True), sink_b)
            p_a = jnp.exp(s_a - m_a)
            p_b = jnp.exp(s_b - m_b)
            d_a = jnp.sum(p_a, axis=-1, keepdims=True) + jnp.exp(sink_a - m_a)
            d_b = jnp.sum(p_b, axis=-1, keepdims=True) + jnp.exp(sink_b - m_b)
            probs.append(jnp.concatenate([p_a, p_b], axis=1).astype(BF16))
            denoms.append(jnp.where(lane_o, d_a, d_b))
        pv = _dot(jnp.concatenate(probs, axis=0), vb)
        for gi, pair in enumerate(pairs):
            o_ref[:, pair * LANES:(pair + 1) * LANES] = (
                pv[gi * blk:(gi + 1) * blk] / denoms[gi]).astype(o_ref.dtype)


def _window_attention(proj, bias, sinks):
    blk = ATTN_BLOCK
    nblk = T_ALL // blk
    k_cb = C_Q_WIDTH // C_KV_WIDTH
    assert C_Q_WIDTH % C_KV_WIDTH == 0
    left = lambda n: jnp.maximum(n - 1, 0)
    right = lambda n: jnp.minimum(n + 1, nblk - 1)

    def kv_specs(cb):
        return [pl.BlockSpec((blk, C_KV_WIDTH), lambda n: (left(n), cb)),
                pl.BlockSpec((blk, C_KV_WIDTH), lambda n: (n, cb)),
                pl.BlockSpec((blk, C_KV_WIDTH), lambda n: (right(n), cb))]

    return pl.pallas_call(
        _attn_body,
        grid=(nblk,),
        in_specs=[pl.BlockSpec((blk, C_Q_WIDTH), lambda n: (n, 0))] + kv_specs(k_cb) + kv_specs(k_cb + 1)
        + [pl.BlockSpec((C_HEADS // 2, blk, 6 * blk), lambda n: (0, 0, 0)),
           pl.BlockSpec(memory_space=pltpu.SMEM)],
        out_specs=pl.BlockSpec((blk, C_Q_WIDTH), lambda n: (n, 0)),
        out_shape=jax.ShapeDtypeStruct((T_ALL, C_Q_WIDTH), BF16),
        compiler_params=_cparams(1),
        name="window_attention",
    )(proj, proj, proj, proj, proj, proj, proj, bias, sinks.astype(F32))


def _pair_head_order():
    order = [0] * C_HEADS
    for h in range(C_HEADS):
        pair, slot = _pair_of_head(h)
        order[2 * pair + slot] = h
    return jnp.array(order, dtype=jnp.int32)


def _mixer_ab(xb, w_in, fourier_g, gate_w2, gate_b, head_norm_g, w_out, dft):
    w_in_b = w_in.astype(BF16)
    proj = _matmul(xb, w_in_b, AB_MAIN, tm=1024, tn=1024, out_dtype=F32, name="ab_proj")
    gates = _matmul(xb, w_in_b[:, AB_MAIN:], 2 * GATE_RANK, tm=1024, tn=2 * GATE_RANK,
                    out_dtype=F32, name="ab_gate_proj")

    chan_c, chan_s, seq_mats = dft
    p, q = _fnet_channel(proj, fourier_g, chan_c, chan_s, tm=512)
    a_out = None
    for row0, nbatch, seq in ((0, BATCH, SEQ), (T_PROMPT, DEC_BATCH, DEC_SEQ)):
        cmat, nsmat = seq_mats[seq]
        a_out = _fnet_seq(cmat, nsmat, p, q, a_out, row0=row0, nbatch=nbatch, seq=seq,
                          tm=min(seq, 512), tn=512)

    zeros = jnp.zeros((GATE_RANK, B_QK_WIDTH), F32)
    w2_f = jnp.concatenate([gate_w2[0].astype(F32), zeros], axis=0).astype(BF16)
    w2_b = jnp.concatenate([zeros, gate_w2[1].astype(F32)], axis=0).astype(BF16)
    o_f = _gla(proj, gates, w2_f, gate_b[0].astype(F32), reverse=False)
    b_out = _gla(proj, gates, w2_b, gate_b[1].astype(F32), reverse=True,
                 fused_inputs=(o_f, head_norm_g.astype(F32)))

    return _matmul_split_lhs(a_out, b_out, w_out.astype(BF16), tm=1024, tn=1024, name="ab_out_proj")


def _mixer_c(xb, w_in, sinks, w_out, bias):
    order = _pair_head_order()
    wq = w_in[:, :C_Q_WIDTH].reshape(D_MODEL, C_HEADS, C_HEAD_DIM)[:, order, :].reshape(D_MODEL, C_Q_WIDTH)
    w_in_p = jnp.concatenate([wq, w_in[:, C_Q_WIDTH:]], axis=1).astype(BF16)
    wo = w_out.reshape(C_HEADS, C_HEAD_DIM, D_MODEL)[order].reshape(C_Q_WIDTH, D_MODEL).astype(BF16)
    n_cols = C_Q_WIDTH + 2 * C_KV_WIDTH
    proj = _matmul(xb, w_in_p, n_cols, tm=1024, tn=1024, out_dtype=BF16, name="c_proj")
    o = _window_attention(proj, bias, sinks)
    return _matmul(o, wo, D_MODEL, tm=1024, tn=1024, out_dtype=F32, name="c_out_proj")


def kernel(x_prompt, x_sample, rel_bias_table, ab_w_in, ab_fourier_g, ab_gate_w2, ab_gate_b, ab_head_norm_g,
           ab_w_out, c_w_in, c_sinks, c_w_out, ffn_w1, ffn_w3, ffn_w2, ln_g, ln_b):
    xs = (x_prompt.reshape(T_PROMPT, D_MODEL).astype(F32), x_sample.reshape(T_SAMPLE, D_MODEL).astype(F32))
    xb = _concat_cast(*xs, tm=512)

    bias = _band_bias(rel_bias_table)
    chan_c, chan_s = _dft_matrices(A_GROUP_DIM, sin_sign=1.0, tr=128)
    seq_mats = {s: _dft_matrices(s, sin_sign=-1.0, tr=128) for s in sorted({SEQ, DEC_SEQ})}
    dft = (chan_c, chan_s, seq_mats)
    w1, w3 = ffn_w1.astype(F32), ffn_w3.astype(F32)
    w2t = _tile_cols(ffn_w2.astype(BF16), 512)
    ln_g, ln_b = ln_g.astype(F32), ln_b.astype(F32)

    for i in range(DEPTH):
        j = i // 2
        if i % 2 == 0:
            h = _mixer_ab(xb, ab_w_in[j], ab_fourier_g[j], ab_gate_w2[j], ab_gate_b[j],
                          ab_head_norm_g[j], ab_w_out[j], dft)
        else:
            h = _mixer_c(xb, c_w_in[j], c_sinks[j], c_w_out[j], bias)
        x, xb = _deepnorm(xs, h, ln_g[i, 0], ln_b[i, 0], tm=256)
        hidden = _swiglu_up(xb, w1, w3, i, tm=2048, tn=256, sub=512)
        f = _ffn_down(hidden, w2t, i, tm=512)
        last = i == DEPTH - 1
        xs = _deepnorm((x,), f, ln_g[i, 1], ln_b[i, 1], tm=256, split_out=last, with_bf16=not last)
        if not last:
            xs, xb = (xs[0],), xs[1]

    return (xs[0].reshape(BATCH, SEQ, D_MODEL), xs[1].reshape(DEC_BATCH, DEC_SEQ, D_MODEL))
```

```python
import functools
import math

import jax
import jax.numpy as jnp
from jax import lax
from jax.experimental import pallas as pl
from jax.experimental.pallas import tpu as pltpu

D_MODEL = 4096
BATCH = 8
SEQ = 2048
DEPTH = 2
DEC_BATCH = 4
DEC_SEQ = 4096

A_GROUPS = 4
A_GROUP_DIM = D_MODEL // 8
A_WIDTH = A_GROUPS * A_GROUP_DIM
B_HEADS = 4
B_DV = D_MODEL // 8
B_DK = B_DV // 2
B_QK_WIDTH = B_HEADS * B_DK
B_V_WIDTH = B_HEADS * B_DV
GATE_RANK = 16
GATE_TEMP = 16.0
AB_MAIN = A_WIDTH + 2 * B_QK_WIDTH + 2 * B_V_WIDTH
C_HEAD_DIM = 64
C_HEADS = D_MODEL // C_HEAD_DIM
C_KV_HEADS = 8
C_GROUP = C_HEADS // C_KV_HEADS
C_Q_WIDTH = C_HEADS * C_HEAD_DIM
C_KV_WIDTH = C_KV_HEADS * C_HEAD_DIM
WINDOW = 128
ATTN_BLOCK = 128
N_BUCKETS = 32
MAX_DISTANCE = 128
D_FF = -(-8 * D_MODEL // (3 * 256)) * 256
DEEPNORM_ALPHA = (2 * DEPTH) ** 0.25
LN_EPS = 1e-5

T_PROMPT = BATCH * SEQ
T_SAMPLE = DEC_BATCH * DEC_SEQ
T_ALL = T_PROMPT + T_SAMPLE

F32 = jnp.float32
BF16 = jnp.bfloat16
NEG_BIG = -1e30

V7X_VMEM_LIMIT_BYTES = 56 * 1024 * 1024
LANES = 128
GLA_CHUNK = 128
GLA_SUB = 16


def _cparams(n_axes):
    return pltpu.CompilerParams(dimension_semantics=("arbitrary",) * n_axes,
                                vmem_limit_bytes=V7X_VMEM_LIMIT_BYTES)


def _fit(tile, dim):
    tile = min(tile, dim)
    assert dim % tile == 0, (tile, dim)
    return tile


def _dot(a, b):
    return jnp.dot(a, b, preferred_element_type=F32)


def _dot_nt(a, b):
    return lax.dot_general(a, b, (((1,), (1,)), ((), ())), preferred_element_type=F32)


def _dot_tn(a, b):
    return lax.dot_general(a, b, (((0,), (0,)), ((), ())), preferred_element_type=F32)


def _standardize(v):
    mu = jnp.mean(v, axis=-1, keepdims=True)
    c = v - mu
    return c * lax.rsqrt(jnp.mean(c * c, axis=-1, keepdims=True) + LN_EPS)


def _sigmoid(v):
    return 1.0 / (1.0 + jnp.exp(-v))


def _mm_body(x_ref, w_ref, o_ref):
    o_ref[...] = _dot(x_ref[...], w_ref[...]).astype(o_ref.dtype)


def _matmul(x, w, n_cols, *, tm, tn, out_dtype, name, col0=0):
    m, k = x.shape
    tm, tn = _fit(tm, m), _fit(tn, n_cols)
    cb = col0 // tn
    return pl.pallas_call(
        _mm_body,
        grid=(m // tm, n_cols // tn),
        in_specs=[pl.BlockSpec((tm, k), lambda i, j: (i, 0)),
                  pl.BlockSpec((k, tn), lambda i, j: (0, j + cb))],
        out_specs=pl.BlockSpec((tm, tn), lambda i, j: (i, j)),
        out_shape=jax.ShapeDtypeStruct((m, n_cols), out_dtype),
        compiler_params=_cparams(2),
        name=name,
    )(x, w)


def _mm2_body(a1_ref, a2_ref, b_ref, w_ref, o_ref, *, n_first):
    ka = a1_ref.shape[1]

    def run(a_ref):
        acc = _dot(a_ref[...], w_ref[:ka, :]) + _dot(b_ref[...], w_ref[ka:, :])
        o_ref[...] = acc.astype(o_ref.dtype)

    i = pl.program_id(0)
    pl.when(i < n_first)(lambda: run(a1_ref))
    pl.when(i >= n_first)(lambda: run(a2_ref))


def _matmul_split_lhs(a_parts, b, w, *, tm, tn, name):
    a1, a2 = a_parts
    (m1, ka), m2 = a1.shape, a2.shape[0]
    m, kb = b.shape
    n = w.shape[1]
    tm, tn = _fit(_fit(tm, m1), m2), _fit(tn, n)
    first, rest = _first_rows(m1 // tm)
    return pl.pallas_call(
        functools.partial(_mm2_body, n_first=m1 // tm),
        grid=(m // tm, n // tn),
        in_specs=[pl.BlockSpec((tm, ka), lambda i, j: first(i)),
                  pl.BlockSpec((tm, ka), lambda i, j: rest(i)),
                  pl.BlockSpec((tm, kb), lambda i, j: (i, 0)),
                  pl.BlockSpec((ka + kb, tn), lambda i, j: (0, j))],
        out_specs=pl.BlockSpec((tm, tn), lambda i, j: (i, j)),
        out_shape=jax.ShapeDtypeStruct((m, n), F32),
        compiler_params=_cparams(2),
        name=name,
    )(a1, a2, b, w)


def _swiglu_rows(x, w_vals):
    a = _dot(x, w_vals[0])
    b = _dot(x, w_vals[1])
    return a * _sigmoid(a) * b


def _plain_rows(x, w_vals):
    return _dot(x, w_vals[0])


def _ln_matmul_body(*refs, n_w, rows_fn, tm, rows, n_chunks, n_first_tiles, rest_row0, sub):
    xf_ref, xr_ref, h_ref, g_ref, b_ref = refs[:5]
    w_refs = refs[5:5 + n_w]
    o_ref, y_ref, xb_ref, xin_ref, hin_ref, yout_ref, sem_x, sem_h, sem_y = refs[5 + n_w:]
    i, j = pl.program_id(0), pl.program_id(1)
    has_next = i + 1 < pl.num_programs(0)
    cur = lax.rem(i, 2)

    def start_in(tile, c, slot):
        r = pl.multiple_of(tile * tm + c * rows, rows)

        @pl.when(tile < n_first_tiles)
        def _():
            pltpu.make_async_copy(xf_ref.at[pl.ds(r, rows)], xin_ref.at[slot], sem_x.at[slot]).start()

        @pl.when(tile >= n_first_tiles)
        def _():
            rr = pl.multiple_of(r - n_first_tiles * tm + rest_row0, rows)
            pltpu.make_async_copy(xr_ref.at[pl.ds(rr, rows)], xin_ref.at[slot], sem_x.at[slot]).start()

        pltpu.make_async_copy(h_ref.at[pl.ds(r, rows)], hin_ref.at[slot], sem_h.at[slot]).start()

    def wait_in(slot):
        pltpu.make_async_copy(xf_ref.at[pl.ds(0, rows)], xin_ref.at[slot], sem_x.at[slot]).wait()
        pltpu.make_async_copy(h_ref.at[pl.ds(0, rows)], hin_ref.at[slot], sem_h.at[slot]).wait()

    def out_copy(tile, c, slot):
        r = pl.multiple_of(tile * tm + c * rows, rows)
        return pltpu.make_async_copy(yout_ref.at[slot], y_ref.at[pl.ds(r, rows)], sem_y.at[slot])

    def ln_chunk(c, slot, xb_slot):
        y = _standardize(DEEPNORM_ALPHA * xin_ref[slot] + hin_ref[slot]) * g_ref[...] + b_ref[...]
        yout_ref[slot] = y
        xb_ref[xb_slot, pl.ds(pl.multiple_of(c * rows, rows), rows), :] = y.astype(BF16)

    @pl.when((i == 0) & (j == 0))
    def _():
        def chunk(c, carry):
            start_in(0, c, 1)
            wait_in(1)
            ln_chunk(c, 1, 0)
            out_copy(0, c, 1).start()
            out_copy(0, c, 1).wait()
            return carry
        lax.fori_loop(0, n_chunks, chunk, 0)

    @pl.when(has_next & (j >= 3) & (j <= n_chunks + 2))
    def _():
        out_copy(0, 0, lax.rem(j + 1, 2)).wait()

    @pl.when(has_next & (j >= 1) & (j <= n_chunks))
    def _():
        wait_in(lax.rem(j - 1, 2))

    @pl.when(has_next & (j < n_chunks))
    def _():
        start_in(i + 1, j, lax.rem(j, 2))

    w_vals = [w_ref[...].astype(BF16) for w_ref in w_refs]
    for r in range(tm // sub):
        x = xb_ref[cur, pl.ds(r * sub, sub), :]
        o_ref[pl.ds(r * sub, sub), :] = rows_fn(x, w_vals).astype(o_ref.dtype)

    c = jnp.where(j == 0, 1, jnp.where(j > n_chunks, n_chunks - 2, j - 1))
    ln_chunk(c, lax.rem(c, 2), 1 - cur)

    @pl.when(has_next & (j >= 1) & (j <= n_chunks))
    def _():
        out_copy(i + 1, j - 1, lax.rem(j - 1, 2)).start()


def _ln_matmul(xs, rest_row0, h, g, b, weights, w_spec_of, n, rows_fn, *, tm, tn, sub, name):
    m, d = h.shape
    tm, tn = _fit(_fit(tm, T_PROMPT), T_SAMPLE), _fit(tn, n)
    n_steps = n // tn
    n_chunks = 1 << ((n_steps - 3).bit_length() - 1)
    rows = tm // n_chunks
    assert n_chunks >= 2 and rows % 16 == 0 and tm % sub == 0
    any_spec = pl.BlockSpec(memory_space=pl.ANY)
    vec = pl.BlockSpec((1, d), lambda i, j: (0, 0))
    out, y = pl.pallas_call(
        functools.partial(_ln_matmul_body, n_w=len(weights), rows_fn=rows_fn, tm=tm, rows=rows,
                          n_chunks=n_chunks, n_first_tiles=T_PROMPT // tm, rest_row0=rest_row0, sub=sub),
        grid=(m // tm, n_steps),
        in_specs=[any_spec, any_spec, any_spec, vec, vec] + [w_spec_of(tn)] * len(weights),
        out_specs=[pl.BlockSpec((tm, tn), lambda i, j: (i, j)), any_spec],
        out_shape=[jax.ShapeDtypeStruct((m, n), BF16), jax.ShapeDtypeStruct((m, d), F32)],
        scratch_shapes=[pltpu.VMEM((2, tm, d), BF16),
                        pltpu.VMEM((2, rows, d), F32), pltpu.VMEM((2, rows, d), F32),
                        pltpu.VMEM((2, rows, d), F32),
                        pltpu.SemaphoreType.DMA((2,)), pltpu.SemaphoreType.DMA((2,)),
                        pltpu.SemaphoreType.DMA((2,))],
        compiler_params=_cparams(2),
        name=name,
    )(xs[0], xs[1], h, g.reshape(1, d), b.reshape(1, d), *weights)
    return y, out


def _ffn_down(x, w2, layer, *, tm, tn):
    m, k = x.shape
    n = w2.shape[2]
    tm, tn = _fit(tm, m), _fit(tn, n)
    return pl.pallas_call(
        _mm_body,
        grid=(m // tm, n // tn),
        in_specs=[pl.BlockSpec((tm, k), lambda i, j: (i, 0)),
                  pl.BlockSpec((None, k, tn), lambda i, j: (layer, 0, j))],
        out_specs=pl.BlockSpec((tm, tn), lambda i, j: (i, j)),
        out_shape=jax.ShapeDtypeStruct((m, n), F32),
        compiler_params=_cparams(2),
        name="ffn_down",
    )(x, w2)


def _first_rows(n_first):
    return (lambda i: (jnp.minimum(i, n_first - 1), 0)), (lambda i: (jnp.maximum(i - n_first, 0), 0))


def _cast_body(xp_ref, xs_ref, o_ref, *, n_first):
    i = pl.program_id(0)

    @pl.when(i < n_first)
    def _():
        o_ref[...] = xp_ref[...].astype(o_ref.dtype)

    @pl.when(i >= n_first)
    def _():
        o_ref[...] = xs_ref[...].astype(o_ref.dtype)


def _concat_cast(x_first, x_rest, *, tm):
    (m1, d), m2 = x_first.shape, x_rest.shape[0]
    tm = _fit(_fit(tm, m1), m2)
    first, rest = _first_rows(m1 // tm)
    return pl.pallas_call(
        functools.partial(_cast_body, n_first=m1 // tm),
        grid=((m1 + m2) // tm,),
        in_specs=[pl.BlockSpec((tm, d), first), pl.BlockSpec((tm, d), rest)],
        out_specs=pl.BlockSpec((tm, d), lambda i: (i, 0)),
        out_shape=jax.ShapeDtypeStruct((m1 + m2, d), BF16),
        compiler_params=_cparams(1),
        name="concat_cast",
    )(x_first, x_rest)


def _ln_body(*refs, n_first, two_in, two_out, with_bf16):
    n_x = 2 if two_in else 1
    x_refs, (h_ref, g_ref, b_ref), out_refs = refs[:n_x], refs[n_x:n_x + 3], refs[n_x + 3:]

    def run(x_ref, y_ref):
        y = _standardize(DEEPNORM_ALPHA * x_ref[...] + h_ref[...]) * g_ref[...] + b_ref[...]
        y_ref[...] = y
        if with_bf16:
            out_refs[-1][...] = y.astype(BF16)

    if not (two_in or two_out):
        run(x_refs[0], out_refs[0])
        return
    i = pl.program_id(0)
    pl.when(i < n_first)(lambda: run(x_refs[0], out_refs[0]))
    pl.when(i >= n_first)(lambda: run(x_refs[-1], out_refs[1 if two_out else 0]))


def _deepnorm(xs, h, g, b, *, tm, split_out=False, with_bf16=True):
    m, d = h.shape
    tm = _fit(_fit(tm, T_PROMPT), T_SAMPLE)
    first, rest = _first_rows(T_PROMPT // tm)
    row = pl.BlockSpec((tm, d), lambda i: (i, 0))
    vec = pl.BlockSpec((1, d), lambda i: (0, 0))
    two_in = len(xs) == 2
    x_specs = [pl.BlockSpec((tm, d), first), pl.BlockSpec((tm, d), rest)] if two_in else [row]
    if split_out:
        out_specs = [pl.BlockSpec((tm, d), first), pl.BlockSpec((tm, d), rest)]
        out_shape = [jax.ShapeDtypeStruct((T_PROMPT, d), F32), jax.ShapeDtypeStruct((T_SAMPLE, d), F32)]
    else:
        out_specs, out_shape = [row], [jax.ShapeDtypeStruct((m, d), F32)]
    if with_bf16:
        out_specs, out_shape = out_specs + [row], out_shape + [jax.ShapeDtypeStruct((m, d), BF16)]
    return pl.pallas_call(
        functools.partial(_ln_body, n_first=T_PROMPT // tm, two_in=two_in, two_out=split_out,
                          with_bf16=with_bf16),
        grid=(m // tm,),
        in_specs=x_specs + [row, vec, vec],
        out_specs=out_specs,
        out_shape=out_shape,
        compiler_params=_cparams(1),
        name="deepnorm",
    )(*xs, h, g.reshape(1, d), b.reshape(1, d))


def _dft_body(c_ref, s_ref, ci_ref, si_ref, *, n, scale, sin_sign):
    tr, nc = c_ref.shape
    i = pl.program_id(0)
    unit = 2.0 * math.pi / n

    @pl.when(i == 0)
    def _():
        row = lax.broadcasted_iota(jnp.int32, (tr, nc), 0)
        col = lax.broadcasted_iota(jnp.int32, (tr, nc), 1)
        ang = ((row * col) & (n - 1)).astype(F32) * unit
        ci_ref[...] = jnp.cos(ang)
        si_ref[...] = jnp.sin(ang)

    col = lax.broadcasted_iota(jnp.int32, (1, nc), 1)
    ang0 = (((i * tr) * col) & (n - 1)).astype(F32) * unit
    c0 = jnp.cos(ang0) * scale
    s0 = jnp.sin(ang0) * scale
    ci = ci_ref[...]
    si = si_ref[...]
    c_ref[...] = (c0 * ci - s0 * si).astype(BF16)
    s_ref[...] = ((s0 * ci + c0 * si) * sin_sign).astype(BF16)


def _dft_matrices(n, *, sin_sign, tr):
    assert n & (n - 1) == 0
    tr = _fit(tr, n)
    blk = pl.BlockSpec((tr, n), lambda i: (i, 0))
    return pl.pallas_call(
        functools.partial(_dft_body, n=n, scale=n ** -0.5, sin_sign=sin_sign),
        grid=(n // tr,),
        out_specs=[blk, blk],
        out_shape=[jax.ShapeDtypeStruct((n, n), BF16)] * 2,
        scratch_shapes=[pltpu.VMEM((tr, n), F32)] * 2,
        compiler_params=_cparams(1),
        name=f"dft_matrices_{n}",
    )()


def _fnet_channel_body(u_ref, g_ref, c_ref, s_ref, p_ref, q_ref):
    for grp in range(A_GROUPS):
        sl = slice(grp * A_GROUP_DIM, (grp + 1) * A_GROUP_DIM)
        un = (_standardize(u_ref[:, sl]) * g_ref[:, sl]).astype(BF16)
        p_ref[:, sl] = _dot(un, c_ref[...]).astype(BF16)
        q_ref[:, sl] = _dot(un, s_ref[...]).astype(BF16)


def _fnet_channel(proj, fourier_g, cmat, smat, *, tm):
    m = proj.shape[0]
    tm = _fit(tm, m)
    row = pl.BlockSpec((tm, A_WIDTH), lambda i: (i, 0))
    mat = pl.BlockSpec((A_GROUP_DIM, A_GROUP_DIM), lambda i: (0, 0))
    return pl.pallas_call(
        _fnet_channel_body,
        grid=(m // tm,),
        in_specs=[row, pl.BlockSpec((1, A_WIDTH), lambda i: (0, 0)), mat, mat],
        out_specs=[row, row],
        out_shape=[jax.ShapeDtypeStruct((m, A_WIDTH), BF16)] * 2,
        compiler_params=_cparams(1),
        name="fnet_channel",
    )(proj, fourier_g.reshape(1, A_WIDTH), cmat, smat)


def _fnet_seq_body(c_ref, s_ref, p_ref, q_ref, o_ref):
    o_ref[...] = (_dot(c_ref[...], p_ref[...]) + _dot(s_ref[...], q_ref[...])).astype(o_ref.dtype)


def _fnet_seq(cmat, nsmat, p, q, *, row0, nbatch, seq, tm, tn):
    assert row0 % seq == 0
    tm, tn = _fit(tm, seq), _fit(tn, A_WIDTH)
    sb = row0 // seq
    mb = seq // tm
    return pl.pallas_call(
        _fnet_seq_body,
        grid=(mb, nbatch, A_WIDTH // tn),
        in_specs=[pl.BlockSpec((tm, seq), lambda m, b, n: (m, 0)),
                  pl.BlockSpec((tm, seq), lambda m, b, n: (m, 0)),
                  pl.BlockSpec((seq, tn), lambda m, b, n: (sb + b, n)),
                  pl.BlockSpec((seq, tn), lambda m, b, n: (sb + b, n))],
        out_specs=pl.BlockSpec((tm, tn), lambda m, b, n: (b * mb + m, n)),
        out_shape=jax.ShapeDtypeStruct((nbatch * seq, A_WIDTH), BF16),
        compiler_params=_cparams(3),
        name=f"fnet_seq_{seq}",
    )(cmat, nsmat, p, q)


def _seq_pos(blk, n_prompt_blocks, blocks_per_prompt, blocks_per_sample):
    in_prompt = blk < n_prompt_blocks
    loc = jnp.where(in_prompt, lax.rem(blk, blocks_per_prompt),
                    lax.rem(blk - n_prompt_blocks, blocks_per_sample))
    return loc, jnp.where(in_prompt, blocks_per_prompt, blocks_per_sample)


def _pad_rows(x, before, after):
    parts = [jnp.zeros((n, x.shape[1]), x.dtype) for n in (before,) if n] + [x]
    parts += [jnp.zeros((n, x.shape[1]), x.dtype) for n in (after,) if n]
    return jnp.concatenate(parts, axis=0) if len(parts) > 1 else x


def _gla_body(q_ref, k_ref, v_ref, gt_ref, w2_ref, gb_ref, *rest, reverse, fuse):
    if fuse:
        of_ref, r_ref, hn_ref, o_ref, st_ref = rest
    else:
        o_ref, st_ref = rest
    c = GLA_CHUNK
    n_chunks = T_ALL // c
    step = pl.program_id(0)
    blk = n_chunks - 1 - step if reverse else step
    loc, nloc = _seq_pos(blk, T_PROMPT // c, SEQ // c, DEC_SEQ // c)
    starts_sequence = (loc == nloc - 1) if reverse else (loc == 0)

    @pl.when(starts_sequence)
    def _():
        st_ref[...] = jnp.zeros_like(st_ref)

    z = _dot(gt_ref[...].astype(BF16), w2_ref[...]) + gb_ref[...]
    la = (jnp.minimum(z, 0.0) - jnp.log(1.0 + jnp.exp(-jnp.abs(z)))) * (1.0 / GATE_TEMP)
    row = lax.broadcasted_iota(jnp.int32, (c, c), 0)
    col = lax.broadcasted_iota(jnp.int32, (c, c), 1)
    causal = (col >= row) if reverse else (col <= row)
    tri = jnp.where(causal, 1.0, 0.0).astype(BF16)
    la_hi = la.astype(BF16)
    rem1 = la - la_hi.astype(F32)
    la_mid = rem1.astype(BF16)
    la_lo = (rem1 - la_mid.astype(F32)).astype(BF16)
    g = _dot(tri, la_hi) + _dot(tri, la_mid) + _dot(tri, la_lo)
    edge = 0 if reverse else c - 1
    g_total = g[edge:edge + 1, :]

    n_sub = c // GLA_SUB
    for h in range(B_HEADS):
        ks = slice(h * B_DK, (h + 1) * B_DK)
        vs = slice(h * B_DV, (h + 1) * B_DV)
        q = q_ref[:, ks] * (B_DK ** -0.5)
        k = k_ref[:, ks]
        v = v_ref[:, vs].astype(BF16)
        gh = g[:, ks]
        gt = g_total[:, ks]
        st = st_ref[h]

        o = _dot_nt((q * jnp.exp(gh)).astype(BF16), st.astype(BF16))
        k_out = (k * jnp.exp(gt - gh)).astype(BF16)

        q_parts, k_parts = [], []
        for j in range(n_sub):
            j0 = j * GLA_SUB
            ref_row = j0 + GLA_SUB - 1 if reverse else j0
            gref = gh[ref_row:ref_row + 1, :]
            lo, hi = (0, j0 + GLA_SUB) if reverse else (j0, c)
            qj = q[lo:hi] * jnp.exp(gh[lo:hi] - gref)
            kj = k[j0:j0 + GLA_SUB] * jnp.exp(gref - gh[j0:j0 + GLA_SUB])
            q_parts.append(_pad_rows(qj, lo, c - hi))
            k_parts.append(_pad_rows(kj, j0, c - j0 - GLA_SUB))
        q_cat = jnp.concatenate(q_parts, axis=1).astype(BF16)
        k_cat = jnp.concatenate(k_parts, axis=1).astype(BF16)
        scores = jnp.where(causal, _dot_nt(q_cat, k_cat), 0.0)
        o = o + _dot(scores.astype(BF16), v)

        st_ref[h] = st * jnp.exp(gt) + _dot_tn(v, k_out)

        if fuse:
            o = _standardize(o + of_ref[:, vs]) * hn_ref[:, vs]
            r = r_ref[:, vs]
            o = o * (r * _sigmoid(r))
        o_ref[:, vs] = o.astype(o_ref.dtype)


def _gla(proj, gates, w2, gb, *, reverse, fused_inputs=None):
    c = GLA_CHUNK
    n_chunks = T_ALL // c
    blk = (lambda t: n_chunks - 1 - t) if reverse else (lambda t: t)
    qk_w, v_w = B_QK_WIDTH, B_V_WIDTH
    q_cb = A_WIDTH // qk_w
    v_cb = (A_WIDTH + 2 * qk_w) // v_w
    assert A_WIDTH % qk_w == 0 and (A_WIDTH + 2 * qk_w) % v_w == 0
    in_specs = [pl.BlockSpec((c, qk_w), lambda t: (blk(t), q_cb)),
                pl.BlockSpec((c, qk_w), lambda t: (blk(t), q_cb + 1)),
                pl.BlockSpec((c, v_w), lambda t: (blk(t), v_cb)),
                pl.BlockSpec((c, 2 * GATE_RANK), lambda t: (blk(t), 0)),
                pl.BlockSpec((2 * GATE_RANK, qk_w), lambda t: (0, 0)),
                pl.BlockSpec((1, qk_w), lambda t: (0, 0))]
    args = [proj, proj, proj, gates, w2, gb.reshape(1, qk_w)]
    fuse = fused_inputs is not None
    if fuse:
        o_other, head_norm_g = fused_inputs
        in_specs += [pl.BlockSpec((c, v_w), lambda t: (blk(t), 0)),
                     pl.BlockSpec((c, v_w), lambda t: (blk(t), v_cb + 1)),
                     pl.BlockSpec((1, v_w), lambda t: (0, 0))]
        args += [o_other, proj, head_norm_g.reshape(1, v_w)]
    return pl.pallas_call(
        functools.partial(_gla_body, reverse=reverse, fuse=fuse),
        grid=(n_chunks,),
        in_specs=in_specs,
        out_specs=pl.BlockSpec((c, v_w), lambda t: (blk(t), 0)),
        out_shape=jax.ShapeDtypeStruct((T_ALL, v_w), BF16 if fuse else F32),
        scratch_shapes=[pltpu.VMEM((B_HEADS, B_DV, B_DK), F32)],
        compiler_params=_cparams(1),
        name="gla_bwd_norm_gate" if fuse else "gla_fwd",
    )(*args)


def _t5_bucket(rel):
    nb = N_BUCKETS // 2
    max_exact = nb // 2
    ret = jnp.where(rel > 0, nb, 0)
    n = jnp.abs(rel)
    nf = jnp.maximum(n, 1).astype(jnp.float32)
    large = max_exact + (jnp.log(nf / max_exact) / math.log(MAX_DISTANCE / max_exact)
                         * (nb - max_exact)).astype(jnp.int32)
    large = jnp.minimum(large, nb - 1)
    return ret + jnp.where(n < max_exact, n, large)


def _bias_body(table_ref, bucket_ref, o_ref):
    h = pl.program_id(0)
    blk, kw = bucket_ref.shape
    bucket = bucket_ref[...]
    qq = lax.broadcasted_iota(jnp.int32, (blk, kw), 0)
    kk = lax.broadcasted_iota(jnp.int32, (blk, kw), 1)
    rel = kk - ATTN_BLOCK - qq
    acc = jnp.zeros((blk, kw), F32)
    for b in range(N_BUCKETS):
        acc = jnp.where(bucket == b, table_ref[b, h], acc)
    o_ref[0] = jnp.where(jnp.abs(rel) <= WINDOW, acc, NEG_BIG)


def _pair_of_head(h):
    kv = h // C_GROUP
    return (kv // 2) * C_GROUP + h % C_GROUP, kv % 2


def _band_bias(table):
    kw = 3 * ATTN_BLOCK
    rel = jnp.arange(kw)[None, :] - ATTN_BLOCK - jnp.arange(ATTN_BLOCK)[:, None]
    bucket = _t5_bucket(rel).astype(jnp.int32)

    def out_index(h):
        pair, slot = _pair_of_head(h)
        return pair, 0, slot

    return pl.pallas_call(
        _bias_body,
        grid=(C_HEADS,),
        in_specs=[pl.BlockSpec(memory_space=pltpu.SMEM),
                  pl.BlockSpec((ATTN_BLOCK, kw), lambda h: (0, 0))],
        out_specs=pl.BlockSpec((1, ATTN_BLOCK, kw), out_index),
        out_shape=jax.ShapeDtypeStruct((C_HEADS // 2, ATTN_BLOCK, 2 * kw), F32),
        compiler_params=_cparams(1),
        name="band_bias",
    )(table.astype(F32), bucket)


def _attn_body(q_ref, kl_ref, kc_ref, kr_ref, vl_ref, vc_ref, vr_ref, bias_ref, sink_ref, o_ref):
    blk = ATTN_BLOCK
    kw = 3 * blk
    n = pl.program_id(0)
    loc, nloc = _seq_pos(n, T_PROMPT // blk, SEQ // blk, DEC_SEQ // blk)
    lane = lax.broadcasted_iota(jnp.int32, (1, 2 * kw), 1)
    kk = jnp.where(lane >= kw, lane - kw, lane)
    outside = ((loc == 0) & (kk < blk)) | ((loc == nloc - 1) & (kk >= 2 * blk))
    penalty = jnp.where(outside, NEG_BIG, 0.0)

    lane_d = lax.broadcasted_iota(jnp.int32, (kw, LANES), 1)
    is_a = lane_d < C_HEAD_DIM
    lane_o = lax.broadcasted_iota(jnp.int32, (blk, LANES), 1) < C_HEAD_DIM
    for m in range(C_KV_HEADS // 2):
        sl = slice(m * LANES, (m + 1) * LANES)
        k3 = jnp.concatenate([kl_ref[:, sl], kc_ref[:, sl], kr_ref[:, sl]], axis=0)
        v3 = jnp.concatenate([vl_ref[:, sl], vc_ref[:, sl], vr_ref[:, sl]], axis=0)
        zero = jnp.zeros_like(k3)
        kb = jnp.concatenate([jnp.where(is_a, k3, zero), jnp.where(is_a, zero, k3)], axis=0)
        vb = jnp.concatenate([jnp.where(is_a, v3, zero), jnp.where(is_a, zero, v3)], axis=0)
        pairs = [m * C_GROUP + gi for gi in range(C_GROUP)]
        q_all = jnp.concatenate([q_ref[:, p * LANES:(p + 1) * LANES] for p in pairs], axis=0)
        s_all = _dot_nt(q_all * (C_HEAD_DIM ** -0.5), kb)
        probs, denoms = [], []
        for gi, pair in enumerate(pairs):
            s = s_all[gi * blk:(gi + 1) * blk] + bias_ref[pair] + penalty
            sink_a = sink_ref[2 * m * C_GROUP + gi]
            sink_b = sink_ref[(2 * m + 1) * C_GROUP + gi]
            s_a, s_b = s[:, :kw], s[:, kw:]
            m_a = jnp.maximum(jnp.max(s_a, axis=-1, keepdims=True), sink_a)
            m_b = jnp.maximum(jnp.max(s_b, axis=-1, keepdims=True), sink_b)
            p_a = jnp.exp(s_a - m_a)
            p_b = jnp.exp(s_b - m_b)
            d_a = jnp.sum(p_a, axis=-1, keepdims=True) + jnp.exp(sink_a - m_a)
            d_b = jnp.sum(p_b, axis=-1, keepdims=True) + jnp.exp(sink_b - m_b)
            probs.append(jnp.concatenate([p_a, p_b], axis=1).astype(BF16))
            denoms.append(jnp.where(lane_o, d_a, d_b))
        pv = _dot(jnp.concatenate(probs, axis=0), vb)
        for gi, pair in enumerate(pairs):
            o_ref[:, pair * LANES:(pair + 1) * LANES] = (
                pv[gi * blk:(gi + 1) * blk] / denoms[gi]).astype(o_ref.dtype)


def _window_attention(proj, bias, sinks):
    blk = ATTN_BLOCK
    nblk = T_ALL // blk
    k_cb = C_Q_WIDTH // C_KV_WIDTH
    assert C_Q_WIDTH % C_KV_WIDTH == 0
    left = lambda n: jnp.maximum(n - 1, 0)
    right = lambda n: jnp.minimum(n + 1, nblk - 1)

    def kv_specs(cb):
        return [pl.BlockSpec((blk, C_KV_WIDTH), lambda n: (left(n), cb)),
                pl.BlockSpec((blk, C_KV_WIDTH), lambda n: (n, cb)),
                pl.BlockSpec((blk, C_KV_WIDTH), lambda n: (right(n), cb))]

    return pl.pallas_call(
        _attn_body,
        grid=(nblk,),
        in_specs=[pl.BlockSpec((blk, C_Q_WIDTH), lambda n: (n, 0))] + kv_specs(k_cb) + kv_specs(k_cb + 1)
        + [pl.BlockSpec((C_HEADS // 2, blk, 6 * blk), lambda n: (0, 0, 0)),
           pl.BlockSpec(memory_space=pltpu.SMEM)],
        out_specs=pl.BlockSpec((blk, C_Q_WIDTH), lambda n: (n, 0)),
        out_shape=jax.ShapeDtypeStruct((T_ALL, C_Q_WIDTH), BF16),
        compiler_params=_cparams(1),
        name="window_attention",
    )(proj, proj, proj, proj, proj, proj, proj, bias, sinks.astype(F32))


def _pair_head_order():
    order = [0] * C_HEADS
    for h in range(C_HEADS):
        pair, slot = _pair_of_head(h)
        order[2 * pair + slot] = h
    return jnp.array(order, dtype=jnp.int32)


def _mixer_ab(xb, w_in, fourier_g, gate_w2, gate_b, head_norm_g, w_out, dft):
    w_in_b = w_in.astype(BF16)
    proj = _matmul(xb, w_in_b, AB_MAIN, tm=1024, tn=1024, out_dtype=F32, name="ab_proj")
    gates = _matmul(xb, w_in_b[:, AB_MAIN:], 2 * GATE_RANK, tm=1024, tn=2 * GATE_RANK,
                    out_dtype=F32, name="ab_gate_proj")

    chan_c, chan_s, seq_mats = dft
    p, q = _fnet_channel(proj, fourier_g, chan_c, chan_s, tm=512)
    a_out = [_fnet_seq(*seq_mats[seq], p, q, row0=row0, nbatch=nbatch, seq=seq, tm=512, tn=512)
             for row0, nbatch, seq in ((0, BATCH, SEQ), (T_PROMPT, DEC_BATCH, DEC_SEQ))]

    zeros = jnp.zeros((GATE_RANK, B_QK_WIDTH), F32)
    w2_f = jnp.concatenate([gate_w2[0].astype(F32), zeros], axis=0).astype(BF16)
    w2_b = jnp.concatenate([zeros, gate_w2[1].astype(F32)], axis=0).astype(BF16)
    o_f = _gla(proj, gates, w2_f, gate_b[0].astype(F32), reverse=False)
    b_out = _gla(proj, gates, w2_b, gate_b[1].astype(F32), reverse=True,
                 fused_inputs=(o_f, head_norm_g.astype(F32)))

    return _matmul_split_lhs(a_out, b_out, w_out.astype(BF16), tm=1024, tn=1024, name="ab_out_proj")


def _mixer_c(x_in, w_in, sinks, w_out, bias):
    order = _pair_head_order()
    wq = w_in[:, :C_Q_WIDTH].reshape(D_MODEL, C_HEADS, C_HEAD_DIM)[:, order, :].reshape(D_MODEL, C_Q_WIDTH)
    w_in_p = jnp.concatenate([wq, w_in[:, C_Q_WIDTH:]], axis=1).astype(BF16)
    wo = w_out.reshape(C_HEADS, C_HEAD_DIM, D_MODEL)[order].reshape(C_Q_WIDTH, D_MODEL).astype(BF16)
    n_cols = C_Q_WIDTH + 2 * C_KV_WIDTH
    if isinstance(x_in, tuple):
        y, proj = _ln_matmul(*x_in, [w_in_p], lambda tn: pl.BlockSpec((D_MODEL, tn), lambda i, j: (0, j)),
                             n_cols, _plain_rows, tm=1024, tn=256, sub=256, name="ln_c_proj")
    else:
        y, proj = None, _matmul(x_in, w_in_p, n_cols, tm=1024, tn=1024, out_dtype=BF16, name="c_proj")
    o = _window_attention(proj, bias, sinks)
    return _matmul(o, wo, D_MODEL, tm=1024, tn=1024, out_dtype=F32, name="c_out_proj"), y


def kernel(x_prompt, x_sample, rel_bias_table, ab_w_in, ab_fourier_g, ab_gate_w2, ab_gate_b, ab_head_norm_g,
           ab_w_out, c_w_in, c_sinks, c_w_out, ffn_w1, ffn_w3, ffn_w2, ln_g, ln_b):
    xs = (x_prompt.reshape(T_PROMPT, D_MODEL).astype(F32), x_sample.reshape(T_SAMPLE, D_MODEL).astype(F32))
    xb = _concat_cast(*xs, tm=512)

    bias = _band_bias(rel_bias_table)
    chan_c, chan_s = _dft_matrices(A_GROUP_DIM, sin_sign=1.0, tr=128)
    seq_mats = {s: _dft_matrices(s, sin_sign=-1.0, tr=128) for s in sorted({SEQ, DEC_SEQ})}
    dft = (chan_c, chan_s, seq_mats)
    w1, w3 = ffn_w1.astype(F32), ffn_w3.astype(F32)
    w2 = ffn_w2.astype(BF16)
    ln_g, ln_b = ln_g.astype(F32), ln_b.astype(F32)

    pending = None
    for i in range(DEPTH):
        j = i // 2
        if i % 2 == 0:
            if pending is not None:
                x, xb = _deepnorm(pending[0][:1], *pending[2:], tm=256)
                xs = (x,)
            h = _mixer_ab(xb, ab_w_in[j], ab_fourier_g[j], ab_gate_w2[j], ab_gate_b[j],
                          ab_head_norm_g[j], ab_w_out[j], dft)
        else:
            h, x = _mixer_c(xb if pending is None else pending, c_w_in[j], c_sinks[j], c_w_out[j], bias)
            if pending is not None:
                xs = (x,)
        pending = None
        two = len(xs) == 2
        x, hidden = _ln_matmul(
            xs if two else (xs[0], xs[0]), 0 if two else T_PROMPT, h, ln_g[i, 0], ln_b[i, 0], [w1, w3],
            lambda tn, layer=i: pl.BlockSpec((None, D_MODEL, tn), lambda r, c: (layer, 0, c)),
            D_FF, _swiglu_rows, tm=1024, tn=256, sub=256, name="ln_swiglu_up")
        f = _ffn_down(hidden, w2, i, tm=512, tn=512)
        if i == DEPTH - 1:
            xs = _deepnorm((x,), f, ln_g[i, 1], ln_b[i, 1], tm=256, split_out=True, with_bf16=False)
        elif (i + 1) % 2 == 1:
            pending = ((x, x), T_PROMPT, f, ln_g[i, 1], ln_b[i, 1])
        else:
            x, xb = _deepnorm((x,), f, ln_g[i, 1], ln_b[i, 1], tm=256)
            xs = (x,)

    return (xs[0].reshape(BATCH, SEQ, D_MODEL), xs[1].reshape(DEC_BATCH, DEC_SEQ, D_MODEL))
```

```python
import functools
import math

import jax
import jax.numpy as jnp
from jax import lax
from jax.experimental import pallas as pl
from jax.experimental.pallas import tpu as pltpu

D_MODEL = 4096
BATCH = 8
SEQ = 2048
DEPTH = 2
DEC_BATCH = 4
DEC_SEQ = 4096

A_GROUPS = 4
A_GROUP_DIM = D_MODEL // 8
A_WIDTH = A_GROUPS * A_GROUP_DIM
B_HEADS = 4
B_DV = D_MODEL // 8
B_DK = B_DV // 2
B_QK_WIDTH = B_HEADS * B_DK
B_V_WIDTH = B_HEADS * B_DV
GATE_RANK = 16
GATE_TEMP = 16.0
AB_MAIN = A_WIDTH + 2 * B_QK_WIDTH + 2 * B_V_WIDTH
C_HEAD_DIM = 64
C_HEADS = D_MODEL // C_HEAD_DIM
C_KV_HEADS = 8
C_GROUP = C_HEADS // C_KV_HEADS
C_Q_WIDTH = C_HEADS * C_HEAD_DIM
C_KV_WIDTH = C_KV_HEADS * C_HEAD_DIM
WINDOW = 128
ATTN_BLOCK = 128
N_BUCKETS = 32
MAX_DISTANCE = 128
D_FF = -(-8 * D_MODEL // (3 * 256)) * 256
DEEPNORM_ALPHA = (2 * DEPTH) ** 0.25
LN_EPS = 1e-5

T_PROMPT = BATCH * SEQ
T_SAMPLE = DEC_BATCH * DEC_SEQ
T_ALL = T_PROMPT + T_SAMPLE

F32 = jnp.float32
BF16 = jnp.bfloat16
NEG_BIG = -1e30
LOG2_E = 1.0 / math.log(2.0)

V7X_VMEM_LIMIT_BYTES = 56 * 1024 * 1024
V7X_VMEM_LIMIT_LARGE_BYTES = 60 * 1024 * 1024
LANES = 128
GLA_CHUNK = 128
GLA_SUB = 16


def _cparams(n_axes, vmem_limit_bytes=V7X_VMEM_LIMIT_BYTES):
    return pltpu.CompilerParams(dimension_semantics=("arbitrary",) * n_axes,
                                vmem_limit_bytes=vmem_limit_bytes)


def _fit(tile, dim):
    tile = min(tile, dim)
    assert dim % tile == 0, (tile, dim)
    return tile


def _dot(a, b):
    return jnp.dot(a, b, preferred_element_type=F32)


def _dot_nt(a, b):
    return lax.dot_general(a, b, (((1,), (1,)), ((), ())), preferred_element_type=F32)


def _dot_tn(a, b):
    return lax.dot_general(a, b, (((0,), (0,)), ((), ())), preferred_element_type=F32)


def _standardize(v):
    mu = jnp.mean(v, axis=-1, keepdims=True)
    c = v - mu
    return c * lax.rsqrt(jnp.mean(c * c, axis=-1, keepdims=True) + LN_EPS)


def _sigmoid(v):
    return 1.0 / (1.0 + jnp.exp(-v))


def _mm_body(x_ref, w_ref, o_ref):
    o_ref[...] = _dot(x_ref[...], w_ref[...]).astype(o_ref.dtype)


def _matmul(x, w, n_cols, *, tm, tn, out_dtype, name, col0=0):
    m, k = x.shape
    tm, tn = _fit(tm, m), _fit(tn, n_cols)
    cb = col0 // tn
    return pl.pallas_call(
        _mm_body,
        grid=(m // tm, n_cols // tn),
        in_specs=[pl.BlockSpec((tm, k), lambda i, j: (i, 0)),
                  pl.BlockSpec((k, tn), lambda i, j: (0, j + cb))],
        out_specs=pl.BlockSpec((tm, tn), lambda i, j: (i, j)),
        out_shape=jax.ShapeDtypeStruct((m, n_cols), out_dtype),
        compiler_params=_cparams(2),
        name=name,
    )(x, w)


def _mm2_body(a1_ref, a2_ref, b_ref, w_ref, o_ref, *, n_first):
    ka = a1_ref.shape[1]

    def run(a_ref):
        acc = _dot(a_ref[...], w_ref[:ka, :]) + _dot(b_ref[...], w_ref[ka:, :])
        o_ref[...] = acc.astype(o_ref.dtype)

    i = pl.program_id(0)
    pl.when(i < n_first)(lambda: run(a1_ref))
    pl.when(i >= n_first)(lambda: run(a2_ref))


def _matmul_split_lhs(a_parts, b, w, *, tm, tn, name):
    a1, a2 = a_parts
    (m1, ka), m2 = a1.shape, a2.shape[0]
    m, kb = b.shape
    n = w.shape[1]
    tm, tn = _fit(_fit(tm, m1), m2), _fit(tn, n)
    first, rest = _first_rows(m1 // tm)
    return pl.pallas_call(
        functools.partial(_mm2_body, n_first=m1 // tm),
        grid=(m // tm, n // tn),
        in_specs=[pl.BlockSpec((tm, ka), lambda i, j: first(i)),
                  pl.BlockSpec((tm, ka), lambda i, j: rest(i)),
                  pl.BlockSpec((tm, kb), lambda i, j: (i, 0)),
                  pl.BlockSpec((ka + kb, tn), lambda i, j: (0, j))],
        out_specs=pl.BlockSpec((tm, tn), lambda i, j: (i, j)),
        out_shape=jax.ShapeDtypeStruct((m, n), F32),
        compiler_params=_cparams(2),
        name=name,
    )(a1, a2, b, w)


def _swiglu_rows(x, w_vals):
    a = _dot(x, w_vals[0])
    b = _dot(x, w_vals[1])
    return a * _sigmoid(a) * b


def _plain_rows(x, w_vals):
    return _dot(x, w_vals[0])


def _ln_matmul_body(*refs, n_w, rows_fn, tm, rows, n_chunks, n_first_tiles, rest_row0, sub):
    xf_ref, xr_ref, h_ref, g_ref, b_ref = refs[:5]
    w_refs = refs[5:5 + n_w]
    o_ref, y_ref, xb0_ref, xb1_ref, xin_ref, hin_ref, yout_ref, sem_x, sem_h, sem_y = refs[5 + n_w:]
    i, j = pl.program_id(0), pl.program_id(1)
    has_next = i + 1 < pl.num_programs(0)
    even_tile = lax.rem(i, 2) == 0

    def start_in(tile, c, slot):
        r = pl.multiple_of(tile * tm + c * rows, rows)

        @pl.when(tile < n_first_tiles)
        def _():
            pltpu.make_async_copy(xf_ref.at[pl.ds(r, rows)], xin_ref.at[slot], sem_x.at[slot]).start()

        @pl.when(tile >= n_first_tiles)
        def _():
            rr = pl.multiple_of(r - n_first_tiles * tm + rest_row0, rows)
            pltpu.make_async_copy(xr_ref.at[pl.ds(rr, rows)], xin_ref.at[slot], sem_x.at[slot]).start()

        pltpu.make_async_copy(h_ref.at[pl.ds(r, rows)], hin_ref.at[slot], sem_h.at[slot]).start()

    def wait_in(slot):
        pltpu.make_async_copy(xf_ref.at[pl.ds(0, rows)], xin_ref.at[slot], sem_x.at[slot]).wait()
        pltpu.make_async_copy(h_ref.at[pl.ds(0, rows)], hin_ref.at[slot], sem_h.at[slot]).wait()

    def out_copy(tile, c, slot):
        r = pl.multiple_of(tile * tm + c * rows, rows)
        return pltpu.make_async_copy(yout_ref.at[slot], y_ref.at[pl.ds(r, rows)], sem_y.at[slot])

    def ln_chunk(c, slot, xb_ref):
        y = _standardize(DEEPNORM_ALPHA * xin_ref[slot] + hin_ref[slot]) * g_ref[...] + b_ref[...]
        yout_ref[slot] = y
        xb_ref[pl.ds(pl.multiple_of(c * rows, rows), rows), :] = y.astype(BF16)

    @pl.when((i == 0) & (j == 0))
    def _():
        def chunk(c, carry):
            start_in(0, c, 1)
            wait_in(1)
            ln_chunk(c, 1, xb0_ref)
            out_copy(0, c, 1).start()
            out_copy(0, c, 1).wait()
            return carry
        lax.fori_loop(0, n_chunks, chunk, 0)

    @pl.when(has_next & (j >= 3) & (j <= n_chunks + 2))
    def _():
        out_copy(0, 0, lax.rem(j + 1, 2)).wait()

    @pl.when(has_next & (j >= 1) & (j <= n_chunks))
    def _():
        wait_in(lax.rem(j - 1, 2))

    @pl.when(has_next & (j < n_chunks))
    def _():
        start_in(i + 1, j, lax.rem(j, 2))

    def compute(xb_cur_ref, xb_next_ref):
        c = jnp.where(j == 0, 1, jnp.where(j > n_chunks, n_chunks - 2, j - 1))
        ln_chunk(c, lax.rem(c, 2), xb_next_ref)
        w_vals = [w_ref[...].astype(BF16) for w_ref in w_refs]
        for r in range(tm // sub):
            x = xb_cur_ref[pl.ds(r * sub, sub), :]
            o_ref[pl.ds(r * sub, sub), :] = rows_fn(x, w_vals).astype(o_ref.dtype)

    pl.when(even_tile)(lambda: compute(xb0_ref, xb1_ref))
    pl.when(jnp.logical_not(even_tile))(lambda: compute(xb1_ref, xb0_ref))

    @pl.when(has_next & (j >= 1) & (j <= n_chunks))
    def _():
        out_copy(i + 1, j - 1, lax.rem(j - 1, 2)).start()


def _ln_matmul(xs, rest_row0, h, g, b, weights, w_spec_of, n, rows_fn, *, tm, tn, sub, name):
    m, d = h.shape
    tm, tn = _fit(_fit(tm, T_PROMPT), T_SAMPLE), _fit(tn, n)
    n_steps = n // tn
    n_chunks = 1 << ((n_steps - 3).bit_length() - 1)
    rows = tm // n_chunks
    assert n_chunks >= 2 and rows % 16 == 0 and tm % sub == 0
    any_spec = pl.BlockSpec(memory_space=pl.ANY)
    vec = pl.BlockSpec((1, d), lambda i, j: (0, 0))
    out, y = pl.pallas_call(
        functools.partial(_ln_matmul_body, n_w=len(weights), rows_fn=rows_fn, tm=tm, rows=rows,
                          n_chunks=n_chunks, n_first_tiles=T_PROMPT // tm, rest_row0=rest_row0, sub=sub),
        grid=(m // tm, n_steps),
        in_specs=[any_spec, any_spec, any_spec, vec, vec] + [w_spec_of(tn)] * len(weights),
        out_specs=[pl.BlockSpec((tm, tn), lambda i, j: (i, j)), any_spec],
        out_shape=[jax.ShapeDtypeStruct((m, n), BF16), jax.ShapeDtypeStruct((m, d), F32)],
        scratch_shapes=[pltpu.VMEM((tm, d), BF16), pltpu.VMEM((tm, d), BF16),
                        pltpu.VMEM((2, rows, d), F32), pltpu.VMEM((2, rows, d), F32),
                        pltpu.VMEM((2, rows, d), F32),
                        pltpu.SemaphoreType.DMA((2,)), pltpu.SemaphoreType.DMA((2,)),
                        pltpu.SemaphoreType.DMA((2,))],
        compiler_params=_cparams(2),
        name=name,
    )(xs[0], xs[1], h, g.reshape(1, d), b.reshape(1, d), *weights)
    return y, out


def _cast_matmul_body(xf_ref, xr_ref, w_ref, wg_ref, o_ref, g_ref, xb0_ref, xb1_ref, xin_ref, sem_x,
                      *, tm, rows, n_chunks, n_first_tiles):
    i, j = pl.program_id(0), pl.program_id(1)
    has_next = i + 1 < pl.num_programs(0)
    even_tile = lax.rem(i, 2) == 0

    def start_in(tile, c, slot):
        r = pl.multiple_of(tile * tm + c * rows, rows)

        @pl.when(tile < n_first_tiles)
        def _():
            pltpu.make_async_copy(xf_ref.at[pl.ds(r, rows)], xin_ref.at[slot], sem_x.at[slot]).start()

        @pl.when(tile >= n_first_tiles)
        def _():
            rr = pl.multiple_of(r - n_first_tiles * tm, rows)
            pltpu.make_async_copy(xr_ref.at[pl.ds(rr, rows)], xin_ref.at[slot], sem_x.at[slot]).start()

    def wait_in(slot):
        pltpu.make_async_copy(xf_ref.at[pl.ds(0, rows)], xin_ref.at[slot], sem_x.at[slot]).wait()

    def cast_chunk(c, slot, xb_ref):
        xb_ref[pl.ds(pl.multiple_of(c * rows, rows), rows), :] = xin_ref[slot].astype(BF16)

    @pl.when((i == 0) & (j == 0))
    def _():
        def chunk(c, carry):
            start_in(0, c, 1)
            wait_in(1)
            cast_chunk(c, 1, xb0_ref)
            return carry
        lax.fori_loop(0, n_chunks, chunk, 0)

    @pl.when(has_next & (j >= 1) & (j <= n_chunks))
    def _():
        wait_in(lax.rem(j - 1, 2))

    @pl.when(has_next & (j < n_chunks))
    def _():
        start_in(i + 1, j, lax.rem(j, 2))

    def compute(xb_cur_ref, xb_next_ref):
        c = jnp.where(j == 0, 1, jnp.where(j > n_chunks, n_chunks - 2, j - 1))
        cast_chunk(c, lax.rem(c, 2), xb_next_ref)
        o_ref[...] = _dot(xb_cur_ref[...], w_ref[...]).astype(o_ref.dtype)

        @pl.when(j == 0)
        def _():
            g_ref[...] = _dot(xb_cur_ref[...], wg_ref[...]).astype(g_ref.dtype)

    pl.when(even_tile)(lambda: compute(xb0_ref, xb1_ref))
    pl.when(jnp.logical_not(even_tile))(lambda: compute(xb1_ref, xb0_ref))


def _cast_matmul(x_first, x_rest, w, n_cols, wg, *, tm, tn):
    (m1, k), m2 = x_first.shape, x_rest.shape[0]
    m, ng = m1 + m2, wg.shape[1]
    tm, tn = _fit(_fit(tm, m1), m2), _fit(tn, n_cols)
    n_steps = n_cols // tn
    n_chunks = 1 << ((n_steps - 1).bit_length() - 1)
    rows = tm // n_chunks
    assert n_chunks >= 2 and rows % 16 == 0
    any_spec = pl.BlockSpec(memory_space=pl.ANY)
    return pl.pallas_call(
        functools.partial(_cast_matmul_body, tm=tm, rows=rows, n_chunks=n_chunks, n_first_tiles=m1 // tm),
        grid=(m // tm, n_steps),
        in_specs=[any_spec, any_spec, pl.BlockSpec((k, tn), lambda i, j: (0, j)),
                  pl.BlockSpec((k, ng), lambda i, j: (0, 0))],
        out_specs=[pl.BlockSpec((tm, tn), lambda i, j: (i, j)), pl.BlockSpec((tm, ng), lambda i, j: (i, 0))],
        out_shape=[jax.ShapeDtypeStruct((m, n_cols), F32), jax.ShapeDtypeStruct((m, ng), F32)],
        scratch_shapes=[pltpu.VMEM((tm, k), BF16), pltpu.VMEM((tm, k), BF16),
                        pltpu.VMEM((2, rows, k), F32), pltpu.SemaphoreType.DMA((2,))],
        compiler_params=_cparams(2, V7X_VMEM_LIMIT_LARGE_BYTES),
        name="ab_proj",
    )(x_first, x_rest, w, wg)


def _ffn_down(x, w2, layer, *, tm, tn):
    m, k = x.shape
    n = w2.shape[2]
    tm, tn = _fit(tm, m), _fit(tn, n)
    return pl.pallas_call(
        _mm_body,
        grid=(m // tm, n // tn),
        in_specs=[pl.BlockSpec((tm, k), lambda i, j: (i, 0)),
                  pl.BlockSpec((None, k, tn), lambda i, j: (layer, 0, j))],
        out_specs=pl.BlockSpec((tm, tn), lambda i, j: (i, j)),
        out_shape=jax.ShapeDtypeStruct((m, n), F32),
        compiler_params=_cparams(2),
        name="ffn_down",
    )(x, w2)


def _first_rows(n_first):
    return (lambda i: (jnp.minimum(i, n_first - 1), 0)), (lambda i: (jnp.maximum(i - n_first, 0), 0))


def _ln_body(*refs, n_first, two_in, two_out, with_bf16):
    n_x = 2 if two_in else 1
    x_refs, (h_ref, g_ref, b_ref), out_refs = refs[:n_x], refs[n_x:n_x + 3], refs[n_x + 3:]

    def run(x_ref, y_ref):
        y = _standardize(DEEPNORM_ALPHA * x_ref[...] + h_ref[...]) * g_ref[...] + b_ref[...]
        y_ref[...] = y
        if with_bf16:
            out_refs[-1][...] = y.astype(BF16)

    if not (two_in or two_out):
        run(x_refs[0], out_refs[0])
        return
    i = pl.program_id(0)
    pl.when(i < n_first)(lambda: run(x_refs[0], out_refs[0]))
    pl.when(i >= n_first)(lambda: run(x_refs[-1], out_refs[1 if two_out else 0]))


def _deepnorm(xs, h, g, b, *, tm, split_out=False, with_bf16=True):
    m, d = h.shape
    tm = _fit(_fit(tm, T_PROMPT), T_SAMPLE)
    first, rest = _first_rows(T_PROMPT // tm)
    row = pl.BlockSpec((tm, d), lambda i: (i, 0))
    vec = pl.BlockSpec((1, d), lambda i: (0, 0))
    two_in = len(xs) == 2
    x_specs = [pl.BlockSpec((tm, d), first), pl.BlockSpec((tm, d), rest)] if two_in else [row]
    if split_out:
        out_specs = [pl.BlockSpec((tm, d), first), pl.BlockSpec((tm, d), rest)]
        out_shape = [jax.ShapeDtypeStruct((T_PROMPT, d), F32), jax.ShapeDtypeStruct((T_SAMPLE, d), F32)]
    else:
        out_specs, out_shape = [row], [jax.ShapeDtypeStruct((m, d), F32)]
    if with_bf16:
        out_specs, out_shape = out_specs + [row], out_shape + [jax.ShapeDtypeStruct((m, d), BF16)]
    return pl.pallas_call(
        functools.partial(_ln_body, n_first=T_PROMPT // tm, two_in=two_in, two_out=split_out,
                          with_bf16=with_bf16),
        grid=(m // tm,),
        in_specs=x_specs + [row, vec, vec],
        out_specs=out_specs,
        out_shape=out_shape,
        compiler_params=_cparams(1),
        name="deepnorm",
    )(*xs, h, g.reshape(1, d), b.reshape(1, d))


def _dft_body(c_ref, s_ref, ci_ref, si_ref, *, n, scale, sin_sign):
    tr, nc = c_ref.shape
    i = pl.program_id(0)
    unit = 2.0 * math.pi / n

    @pl.when(i == 0)
    def _():
        row = lax.broadcasted_iota(jnp.int32, (tr, nc), 0)
        col = lax.broadcasted_iota(jnp.int32, (tr, nc), 1)
        ang = ((row * col) & (n - 1)).astype(F32) * unit
        ci_ref[...] = jnp.cos(ang)
        si_ref[...] = jnp.sin(ang)

    col = lax.broadcasted_iota(jnp.int32, (1, nc), 1)
    ang0 = (((i * tr) * col) & (n - 1)).astype(F32) * unit
    c0 = jnp.cos(ang0) * scale
    s0 = jnp.sin(ang0) * scale
    ci = ci_ref[...]
    si = si_ref[...]
    c_ref[...] = (c0 * ci - s0 * si).astype(BF16)
    s_ref[...] = ((s0 * ci + c0 * si) * sin_sign).astype(BF16)


def _dft_matrices(n, *, sin_sign, tr):
    assert n & (n - 1) == 0
    tr = _fit(tr, n)
    blk = pl.BlockSpec((tr, n), lambda i: (i, 0))
    return pl.pallas_call(
        functools.partial(_dft_body, n=n, scale=n ** -0.5, sin_sign=sin_sign),
        grid=(n // tr,),
        out_specs=[blk, blk],
        out_shape=[jax.ShapeDtypeStruct((n, n), BF16)] * 2,
        scratch_shapes=[pltpu.VMEM((tr, n), F32)] * 2,
        compiler_params=_cparams(1),
        name=f"dft_matrices_{n}",
    )()


def _fnet_channel_body(u_ref, g_ref, c_ref, s_ref, p_ref, q_ref):
    for grp in range(A_GROUPS):
        sl = slice(grp * A_GROUP_DIM, (grp + 1) * A_GROUP_DIM)
        un = (_standardize(u_ref[:, sl]) * g_ref[:, sl]).astype(BF16)
        p_ref[:, sl] = _dot(un, c_ref[...]).astype(BF16)
        q_ref[:, sl] = _dot(un, s_ref[...]).astype(BF16)


def _fnet_channel(proj, fourier_g, cmat, smat, *, tm):
    m = proj.shape[0]
    tm = _fit(tm, m)
    row = pl.BlockSpec((tm, A_WIDTH), lambda i: (i, 0))
    mat = pl.BlockSpec((A_GROUP_DIM, A_GROUP_DIM), lambda i: (0, 0))
    return pl.pallas_call(
        _fnet_channel_body,
        grid=(m // tm,),
        in_specs=[row, pl.BlockSpec((1, A_WIDTH), lambda i: (0, 0)), mat, mat],
        out_specs=[row, row],
        out_shape=[jax.ShapeDtypeStruct((m, A_WIDTH), BF16)] * 2,
        compiler_params=_cparams(1),
        name="fnet_channel",
    )(proj, fourier_g.reshape(1, A_WIDTH), cmat, smat)


def _fnet_seq_body(c_ref, s_ref, p_ref, q_ref, o_ref):
    o_ref[...] = (_dot(c_ref[...], p_ref[...]) + _dot(s_ref[...], q_ref[...])).astype(o_ref.dtype)


def _fnet_seq(cmat, nsmat, p, q, *, row0, nbatch, seq, tm, tn):
    assert row0 % seq == 0
    tm, tn = _fit(tm, seq), _fit(tn, A_WIDTH)
    sb = row0 // seq
    mb = seq // tm
    return pl.pallas_call(
        _fnet_seq_body,
        grid=(mb, nbatch, A_WIDTH // tn),
        in_specs=[pl.BlockSpec((tm, seq), lambda m, b, n: (m, 0)),
                  pl.BlockSpec((tm, seq), lambda m, b, n: (m, 0)),
                  pl.BlockSpec((seq, tn), lambda m, b, n: (sb + b, n)),
                  pl.BlockSpec((seq, tn), lambda m, b, n: (sb + b, n))],
        out_specs=pl.BlockSpec((tm, tn), lambda m, b, n: (b * mb + m, n)),
        out_shape=jax.ShapeDtypeStruct((nbatch * seq, A_WIDTH), BF16),
        compiler_params=_cparams(3),
        name=f"fnet_seq_{seq}",
    )(cmat, nsmat, p, q)


def _seq_pos(blk, n_prompt_blocks, blocks_per_prompt, blocks_per_sample):
    in_prompt = blk < n_prompt_blocks
    loc = jnp.where(in_prompt, lax.rem(blk, blocks_per_prompt),
                    lax.rem(blk - n_prompt_blocks, blocks_per_sample))
    return loc, jnp.where(in_prompt, blocks_per_prompt, blocks_per_sample)


def _pad_rows(x, before, after):
    parts = [jnp.zeros((n, x.shape[1]), x.dtype) for n in (before,) if n] + [x]
    parts += [jnp.zeros((n, x.shape[1]), x.dtype) for n in (after,) if n]
    return jnp.concatenate(parts, axis=0) if len(parts) > 1 else x


def _gla_body(q_ref, k_ref, v_ref, gt_ref, w2_ref, gb_ref, *rest, reverse, fuse):
    if fuse:
        of_ref, r_ref, hn_ref, o_ref, st_ref = rest
    else:
        o_ref, st_ref = rest
    c = GLA_CHUNK
    n_chunks = T_ALL // c
    step = pl.program_id(0)
    blk = n_chunks - 1 - step if reverse else step
    loc, nloc = _seq_pos(blk, T_PROMPT // c, SEQ // c, DEC_SEQ // c)
    starts_sequence = (loc == nloc - 1) if reverse else (loc == 0)

    @pl.when(starts_sequence)
    def _():
        st_ref[...] = jnp.zeros_like(st_ref)

    z = _dot(gt_ref[...].astype(BF16), w2_ref[...]) + gb_ref[...]
    la = (jnp.minimum(z, 0.0) - jnp.log(1.0 + jnp.exp(-jnp.abs(z)))) * (1.0 / GATE_TEMP)
    row = lax.broadcasted_iota(jnp.int32, (c, c), 0)
    col = lax.broadcasted_iota(jnp.int32, (c, c), 1)
    causal = (col >= row) if reverse else (col <= row)
    tri = jnp.where(causal, 1.0, 0.0).astype(BF16)
    la_hi = la.astype(BF16)
    rem1 = la - la_hi.astype(F32)
    la_mid = rem1.astype(BF16)
    la_lo = (rem1 - la_mid.astype(F32)).astype(BF16)
    g = _dot(tri, la_hi) + _dot(tri, la_mid) + _dot(tri, la_lo)
    edge = 0 if reverse else c - 1
    g_total = g[edge:edge + 1, :]

    n_sub = c // GLA_SUB
    for h in range(B_HEADS):
        ks = slice(h * B_DK, (h + 1) * B_DK)
        vs = slice(h * B_DV, (h + 1) * B_DV)
        q = q_ref[:, ks] * (B_DK ** -0.5)
        k = k_ref[:, ks]
        v = v_ref[:, vs].astype(BF16)
        gh = g[:, ks]
        gt = g_total[:, ks]
        st = st_ref[h]

        o = _dot_nt((q * jnp.exp(gh)).astype(BF16), st.astype(BF16))
        k_out = (k * jnp.exp(gt - gh)).astype(BF16)

        q_parts, k_parts = [], []
        for j in range(n_sub):
            j0 = j * GLA_SUB
            ref_row = j0 + GLA_SUB - 1 if reverse else j0
            gref = gh[ref_row:ref_row + 1, :]
            lo, hi = (0, j0 + GLA_SUB) if reverse else (j0, c)
            qj = q[lo:hi] * jnp.exp(gh[lo:hi] - gref)
            kj = k[j0:j0 + GLA_SUB] * jnp.exp(gref - gh[j0:j0 + GLA_SUB])
            q_parts.append(_pad_rows(qj, lo, c - hi))
            k_parts.append(_pad_rows(kj, j0, c - j0 - GLA_SUB))
        q_cat = jnp.concatenate(q_parts, axis=1).astype(BF16)
        k_cat = jnp.concatenate(k_parts, axis=1).astype(BF16)
        scores = jnp.where(causal, _dot_nt(q_cat, k_cat), 0.0)
        o = o + _dot(scores.astype(BF16), v)

        st_ref[h] = st * jnp.exp(gt) + _dot_tn(v, k_out)

        if fuse:
            o = _standardize(o + of_ref[:, vs]) * hn_ref[:, vs]
            r = r_ref[:, vs]
            o = o * (r * _sigmoid(r))
        o_ref[:, vs] = o.astype(o_ref.dtype)


def _gla(proj, gates, w2, gb, *, reverse, fused_inputs=None):
    c = GLA_CHUNK
    n_chunks = T_ALL // c
    blk = (lambda t: n_chunks - 1 - t) if reverse else (lambda t: t)
    qk_w, v_w = B_QK_WIDTH, B_V_WIDTH
    q_cb = A_WIDTH // qk_w
    v_cb = (A_WIDTH + 2 * qk_w) // v_w
    assert A_WIDTH % qk_w == 0 and (A_WIDTH + 2 * qk_w) % v_w == 0
    in_specs = [pl.BlockSpec((c, qk_w), lambda t: (blk(t), q_cb)),
                pl.BlockSpec((c, qk_w), lambda t: (blk(t), q_cb + 1)),
                pl.BlockSpec((c, v_w), lambda t: (blk(t), v_cb)),
                pl.BlockSpec((c, 2 * GATE_RANK), lambda t: (blk(t), 0)),
                pl.BlockSpec((2 * GATE_RANK, qk_w), lambda t: (0, 0)),
                pl.BlockSpec((1, qk_w), lambda t: (0, 0))]
    args = [proj, proj, proj, gates, w2, gb.reshape(1, qk_w)]
    fuse = fused_inputs is not None
    if fuse:
        o_other, head_norm_g = fused_inputs
        in_specs += [pl.BlockSpec((c, v_w), lambda t: (blk(t), 0)),
                     pl.BlockSpec((c, v_w), lambda t: (blk(t), v_cb + 1)),
                     pl.BlockSpec((1, v_w), lambda t: (0, 0))]
        args += [o_other, proj, head_norm_g.reshape(1, v_w)]
    return pl.pallas_call(
        functools.partial(_gla_body, reverse=reverse, fuse=fuse),
        grid=(n_chunks,),
        in_specs=in_specs,
        out_specs=pl.BlockSpec((c, v_w), lambda t: (blk(t), 0)),
        out_shape=jax.ShapeDtypeStruct((T_ALL, v_w), BF16 if fuse else F32),
        scratch_shapes=[pltpu.VMEM((B_HEADS, B_DV, B_DK), F32)],
        compiler_params=_cparams(1),
        name="gla_bwd_norm_gate" if fuse else "gla_fwd",
    )(*args)


def _t5_bucket(rel):
    nb = N_BUCKETS // 2
    max_exact = nb // 2
    ret = jnp.where(rel > 0, nb, 0)
    n = jnp.abs(rel)
    nf = jnp.maximum(n, 1).astype(jnp.float32)
    large = max_exact + (jnp.log(nf / max_exact) / math.log(MAX_DISTANCE / max_exact)
                         * (nb - max_exact)).astype(jnp.int32)
    large = jnp.minimum(large, nb - 1)
    return ret + jnp.where(n < max_exact, n, large)


def _bias_body(table_ref, bucket_ref, o_ref):
    h = pl.program_id(0)
    blk, kw = bucket_ref.shape
    bucket = bucket_ref[...]
    qq = lax.broadcasted_iota(jnp.int32, (blk, kw), 0)
    kk = lax.broadcasted_iota(jnp.int32, (blk, kw), 1)
    rel = kk - ATTN_BLOCK - qq
    acc = jnp.zeros((blk, kw), F32)
    for b in range(N_BUCKETS):
        acc = jnp.where(bucket == b, table_ref[b, h], acc)
    o_ref[0] = jnp.where(jnp.abs(rel) <= WINDOW, acc * LOG2_E, NEG_BIG)


def _pair_of_head(h):
    kv = h // C_GROUP
    return (kv // 2) * C_GROUP + h % C_GROUP, kv % 2


def _band_bias(table):
    kw = 3 * ATTN_BLOCK
    rel = jnp.arange(kw)[None, :] - ATTN_BLOCK - jnp.arange(ATTN_BLOCK)[:, None]
    bucket = _t5_bucket(rel).astype(jnp.int32)

    def out_index(h):
        pair, slot = _pair_of_head(h)
        return pair, 0, slot

    return pl.pallas_call(
        _bias_body,
        grid=(C_HEADS,),
        in_specs=[pl.BlockSpec(memory_space=pltpu.SMEM),
                  pl.BlockSpec((ATTN_BLOCK, kw), lambda h: (0, 0))],
        out_specs=pl.BlockSpec((1, ATTN_BLOCK, kw), out_index),
        out_shape=jax.ShapeDtypeStruct((C_HEADS // 2, ATTN_BLOCK, 2 * kw), F32),
        compiler_params=_cparams(1),
        name="band_bias",
    )(table.astype(F32), bucket)


def _attn_body(q_ref, kl_ref, kc_ref, kr_ref, vl_ref, vc_ref, vr_ref, bias_ref, sink_ref, o_ref):
    blk = ATTN_BLOCK
    kw = 3 * blk
    n = pl.program_id(0)
    loc, nloc = _seq_pos(n, T_PROMPT // blk, SEQ // blk, DEC_SEQ // blk)
    lane = lax.broadcasted_iota(jnp.int32, (1, 2 * kw), 1)
    kk = jnp.where(lane >= kw, lane - kw, lane)
    outside = ((loc == 0) & (kk < blk)) | ((loc == nloc - 1) & (kk >= 2 * blk))
    penalty = jnp.where(outside, NEG_BIG, 0.0)

    lane_d = lax.broadcasted_iota(jnp.int32, (kw, LANES), 1)
    is_a = lane_d < C_HEAD_DIM
    lane_o = lax.broadcasted_iota(jnp.int32, (blk, LANES), 1) < C_HEAD_DIM
    for m in range(C_KV_HEADS // 2):
        sl = slice(m * LANES, (m + 1) * LANES)
        k3 = jnp.concatenate([kl_ref[:, sl], kc_ref[:, sl], kr_ref[:, sl]], axis=0)
        v3 = jnp.concatenate([vl_ref[:, sl], vc_ref[:, sl], vr_ref[:, sl]], axis=0)
        zero = jnp.zeros_like(k3)
        kb = jnp.concatenate([jnp.where(is_a, k3, zero), jnp.where(is_a, zero, k3)], axis=0)
        vb = jnp.concatenate([jnp.where(is_a, v3, zero), jnp.where(is_a, zero, v3)], axis=0)
        pairs = [m * C_GROUP + gi for gi in range(C_GROUP)]
        q_all = jnp.concatenate([q_ref[:, p * LANES:(p + 1) * LANES] for p in pairs], axis=0)
        s_all = _dot_nt(q_all * (C_HEAD_DIM ** -0.5 * LOG2_E), kb)
        probs, denoms = [], []
        for gi, pair in enumerate(pairs):
            s = s_all[gi * blk:(gi + 1) * blk] + bias_ref[pair] + penalty
            sink_a = sink_ref[2 * m * C_GROUP + gi] * LOG2_E
            sink_b = sink_ref[(2 * m + 1) * C_GROUP + gi] * LOG2_E
            s_a, s_b = s[:, :kw], s[:, kw:]
            m_a = jnp.maximum(jnp.max(s_a, axis=-1, keepdims=True), sink_a)
            m_b = jnp.maximum(jnp.max(s_b, axis=-1, keepdims=True), sink_b)
            p_a = jnp.exp2(s_a - m_a)
            p_b = jnp.exp2(s_b - m_b)
            d_a = jnp.sum(p_a, axis=-1, keepdims=True) + jnp.exp2(sink_a - m_a)
            d_b = jnp.sum(p_b, axis=-1, keepdims=True) + jnp.exp2(sink_b - m_b)
            probs.append(jnp.concatenate([p_a, p_b], axis=1).astype(BF16))
            denoms.append(jnp.where(lane_o, d_a, d_b))
        pv = _dot(jnp.concatenate(probs, axis=0), vb)
        for gi, pair in enumerate(pairs):
            o_ref[:, pair * LANES:(pair + 1) * LANES] = (
                pv[gi * blk:(gi + 1) * blk] / denoms[gi]).astype(o_ref.dtype)


def _window_attention(proj, bias, sinks):
    blk = ATTN_BLOCK
    nblk = T_ALL // blk
    k_cb = C_Q_WIDTH // C_KV_WIDTH
    assert C_Q_WIDTH % C_KV_WIDTH == 0
    left = lambda n: jnp.maximum(n - 1, 0)
    right = lambda n: jnp.minimum(n + 1, nblk - 1)

    def kv_specs(cb):
        return [pl.BlockSpec((blk, C_KV_WIDTH), lambda n: (left(n), cb)),
                pl.BlockSpec((blk, C_KV_WIDTH), lambda n: (n, cb)),
                pl.BlockSpec((blk, C_KV_WIDTH), lambda n: (right(n), cb))]

    return pl.pallas_call(
        _attn_body,
        grid=(nblk,),
        in_specs=[pl.BlockSpec((blk, C_Q_WIDTH), lambda n: (n, 0))] + kv_specs(k_cb) + kv_specs(k_cb + 1)
        + [pl.BlockSpec((C_HEADS // 2, blk, 6 * blk), lambda n: (0, 0, 0)),
           pl.BlockSpec(memory_space=pltpu.SMEM)],
        out_specs=pl.BlockSpec((blk, C_Q_WIDTH), lambda n: (n, 0)),
        out_shape=jax.ShapeDtypeStruct((T_ALL, C_Q_WIDTH), BF16),
        compiler_params=_cparams(1),
        name="window_attention",
    )(proj, proj, proj, proj, proj, proj, proj, bias, sinks.astype(F32))


def _pair_head_order():
    order = [0] * C_HEADS
    for h in range(C_HEADS):
        pair, slot = _pair_of_head(h)
        order[2 * pair + slot] = h
    return jnp.array(order, dtype=jnp.int32)


def _mixer_ab(x_in, w_in, fourier_g, gate_w2, gate_b, head_norm_g, w_out, dft):
    w_in_b = w_in.astype(BF16)
    if isinstance(x_in, tuple):
        proj, gates = _cast_matmul(*x_in, w_in_b, AB_MAIN, w_in_b[:, AB_MAIN:], tm=1024, tn=1024)
    else:
        proj = _matmul(x_in, w_in_b, AB_MAIN, tm=1024, tn=1024, out_dtype=F32, name="ab_proj")
        gates = _matmul(x_in, w_in_b[:, AB_MAIN:], 2 * GATE_RANK, tm=1024, tn=2 * GATE_RANK,
                        out_dtype=F32, name="ab_gate_proj")

    chan_c, chan_s, seq_mats = dft
    p, q = _fnet_channel(proj, fourier_g, chan_c, chan_s, tm=512)
    a_out = [_fnet_seq(*seq_mats[seq], p, q, row0=row0, nbatch=nbatch, seq=seq, tm=512, tn=512)
             for row0, nbatch, seq in ((0, BATCH, SEQ), (T_PROMPT, DEC_BATCH, DEC_SEQ))]

    zeros = jnp.zeros((GATE_RANK, B_QK_WIDTH), F32)
    w2_f = jnp.concatenate([gate_w2[0].astype(F32), zeros], axis=0).astype(BF16)
    w2_b = jnp.concatenate([zeros, gate_w2[1].astype(F32)], axis=0).astype(BF16)
    o_f = _gla(proj, gates, w2_f, gate_b[0].astype(F32), reverse=False)
    b_out = _gla(proj, gates, w2_b, gate_b[1].astype(F32), reverse=True,
                 fused_inputs=(o_f, head_norm_g.astype(F32)))

    return _matmul_split_lhs(a_out, b_out, w_out.astype(BF16), tm=1024, tn=1024, name="ab_out_proj")


def _mixer_c(x_in, w_in, sinks, w_out, bias):
    order = _pair_head_order()
    wq = w_in[:, :C_Q_WIDTH].reshape(D_MODEL, C_HEADS, C_HEAD_DIM)[:, order, :].reshape(D_MODEL, C_Q_WIDTH)
    w_in_p = jnp.concatenate([wq, w_in[:, C_Q_WIDTH:]], axis=1).astype(BF16)
    wo = w_out.reshape(C_HEADS, C_HEAD_DIM, D_MODEL)[order].reshape(C_Q_WIDTH, D_MODEL).astype(BF16)
    n_cols = C_Q_WIDTH + 2 * C_KV_WIDTH
    if isinstance(x_in, tuple):
        y, proj = _ln_matmul(*x_in, [w_in_p], lambda tn: pl.BlockSpec((D_MODEL, tn), lambda i, j: (0, j)),
                             n_cols, _plain_rows, tm=1024, tn=256, sub=256, name="ln_c_proj")
    else:
        y, proj = None, _matmul(x_in, w_in_p, n_cols, tm=1024, tn=1024, out_dtype=BF16, name="c_proj")
    o = _window_attention(proj, bias, sinks)
    return _matmul(o, wo, D_MODEL, tm=1024, tn=1024, out_dtype=F32, name="c_out_proj"), y


def kernel(x_prompt, x_sample, rel_bias_table, ab_w_in, ab_fourier_g, ab_gate_w2, ab_gate_b, ab_head_norm_g,
           ab_w_out, c_w_in, c_sinks, c_w_out, ffn_w1, ffn_w3, ffn_w2, ln_g, ln_b):
    xs = (x_prompt.reshape(T_PROMPT, D_MODEL).astype(F32), x_sample.reshape(T_SAMPLE, D_MODEL).astype(F32))
    xb = None

    bias = _band_bias(rel_bias_table)
    chan_c, chan_s = _dft_matrices(A_GROUP_DIM, sin_sign=1.0, tr=128)
    seq_mats = {s: _dft_matrices(s, sin_sign=-1.0, tr=128) for s in sorted({SEQ, DEC_SEQ})}
    dft = (chan_c, chan_s, seq_mats)
    w1, w3 = ffn_w1.astype(F32), ffn_w3.astype(F32)
    w2 = ffn_w2.astype(BF16)
    ln_g, ln_b = ln_g.astype(F32), ln_b.astype(F32)

    pending = None
    for i in range(DEPTH):
        j = i // 2
        if i % 2 == 0:
            if pending is not None:
                x, xb = _deepnorm(pending[0][:1], *pending[2:], tm=256)
                xs = (x,)
            h = _mixer_ab(xs if i == 0 else xb, ab_w_in[j], ab_fourier_g[j], ab_gate_w2[j], ab_gate_b[j],
                          ab_head_norm_g[j], ab_w_out[j], dft)
        else:
            h, x = _mixer_c(xb if pending is None else pending, c_w_in[j], c_sinks[j], c_w_out[j], bias)
            if pending is not None:
                xs = (x,)
        pending = None
        two = len(xs) == 2
        x, hidden = _ln_matmul(
            xs if two else (xs[0], xs[0]), 0 if two else T_PROMPT, h, ln_g[i, 0], ln_b[i, 0], [w1, w3],
            lambda tn, layer=i: pl.BlockSpec((None, D_MODEL, tn), lambda r, c: (layer, 0, c)),
            D_FF, _swiglu_rows, tm=1024, tn=256, sub=256, name="ln_swiglu_up")
        f = _ffn_down(hidden, w2, i, tm=512, tn=512)
        if i == DEPTH - 1:
            xs = _deepnorm((x,), f, ln_g[i, 1], ln_b[i, 1], tm=256, split_out=True, with_bf16=False)
        elif (i + 1) % 2 == 1:
            pending = ((x, x), T_PROMPT, f, ln_g[i, 1], ln_b[i, 1])
        else:
            x, xb = _deepnorm((x,), f, ln_g[i, 1], ln_b[i, 1], tm=256)
            xs = (x,)

    return (xs[0].reshape(BATCH, SEQ, D_MODEL), xs[1].reshape(DEC_BATCH, DEC_SEQ, D_MODEL))
```
